```python
import jax, jax.numpy as jnp
from jax import lax
import numpy as np

D_MODEL = 2048
BATCH = 1
SEQ = 16384
DEPTH = 1
DEC_BATCH = 16
DEC_SEQ = 32
PAST_LEN = 4096

CHUNK = 64
N_HEADS = 16
N_KV = 4
GQA = N_HEADS // N_KV
HEAD_DIM = 64
ATT_DIM = N_HEADS * HEAD_DIM
KV_DIM = N_KV * HEAD_DIM
WINDOW = 128
WIN_CHUNKS = WINDOW // CHUNK
GM_BLOCK = 128
GM_DIM = 1024
GM_GROUPS = 8
GM_GDIM = GM_DIM // GM_GROUPS
N_KEYS = 128
N_EXPERTS = N_KEYS * N_KEYS
PK_HEADS = 8
PK_HALF = 128
PK_DIM = 2 * PK_HALF
PK_TOPK = 16
PEER_BLOCK = 128
EPS = 1e-6
NEG = -1e30

SPLITS = (ATT_DIM,
          ATT_DIM + KV_DIM,
          ATT_DIM + 2 * KV_DIM,
          ATT_DIM + 2 * KV_DIM + GM_DIM,
          ATT_DIM + 2 * KV_DIM + 2 * GM_DIM,
          ATT_DIM + 2 * KV_DIM + 2 * GM_DIM + D_MODEL)
IN_DIM = SPLITS[-1] + D_MODEL

kernel_name = "chunk_stream_swa_gmlp_peer_step"


def rms_norm(x, g):
    xf = x.astype(jnp.float32)
    y = xf * lax.rsqrt(jnp.mean(xf * xf, axis=-1, keepdims=True) + EPS)
    return (y * g.astype(jnp.float32)).astype(x.dtype)


def layer_norm(x, g, b):
    xf = x.astype(jnp.float32)
    mu = jnp.mean(xf, axis=-1, keepdims=True)
    var = jnp.mean(jnp.square(xf - mu), axis=-1, keepdims=True)
    y = (xf - mu) * lax.rsqrt(var + EPS) * g.astype(jnp.float32) + b.astype(jnp.float32)
    return y.astype(x.dtype)


def attend_with_sinks(q, k, v, sinks, mask=None):
    s = jnp.einsum('...qkgd,...skd->...kgqs', q, k).astype(jnp.float32) * (HEAD_DIM ** -0.5)
    if mask is not None:
        s = jnp.where(mask, s, jnp.float32(NEG))
    sk = sinks.astype(jnp.float32).reshape(N_KV, GQA)[:, :, None, None]
    m = jnp.maximum(jnp.max(s, axis=-1, keepdims=True), sk)
    p = jnp.exp(s - m)
    p = p / (jnp.sum(p, axis=-1, keepdims=True) + jnp.exp(sk - m))
    return jnp.einsum('...kgqs,...skd->...qkgd', p.astype(v.dtype), v)


def swa_prompt(q, k, v, sinks):
    b, s = q.shape[:2]
    nc = s // CHUNK
    qc = q.reshape(b, nc, CHUNK, N_KV, GQA, HEAD_DIM)

    def band(t):
        tc = t.reshape(b, nc, CHUNK, N_KV, HEAD_DIM)
        tp = jnp.pad(tc, ((0, 0), (WIN_CHUNKS, 0), (0, 0), (0, 0), (0, 0)))
        return jnp.concatenate([tp[:, i:i + nc] for i in range(WIN_CHUNKS + 1)], axis=2)

    kb, vb = band(k), band(v)
    key_chunk = jnp.arange(nc)[:, None] - WIN_CHUNKS + jnp.arange(WIN_CHUNKS + 1)[None, :]
    valid = jnp.repeat(key_chunk >= 0, CHUNK, axis=1)
    mask = valid[None, :, None, None, None, :]
    o = attend_with_sinks(qc, kb, vb, sinks, mask)
    return o.reshape(b, s, ATT_DIM)


def swa_sample(q, k, v, cache_k, cache_v, sinks):
    db, t = q.shape[:2]
    qg = q.reshape(db, t, N_KV, GQA, HEAD_DIM)
    kk = jnp.concatenate([cache_k, k], axis=1)
    vv = jnp.concatenate([cache_v, v], axis=1)
    return attend_with_sinks(qg, kk, vv, sinks).reshape(db, t, ATT_DIM)


def gmlp_mask():
    i = jnp.arange(GM_BLOCK)
    return (i[None, :] // CHUNK) <= (i[:, None] // CHUNK)


def gmlp_prompt(u, vg, ws, bias):
    b, s = u.shape[:2]
    vb = vg.reshape(b, s // GM_BLOCK, GM_BLOCK, GM_GROUPS, GM_GDIM)
    w = jnp.where(gmlp_mask()[None], ws, 0)
    mixed = jnp.einsum('gij,bnjgc->bnigc', w, vb) + bias.T[:, :, None]
    return u * mixed.reshape(b, s, GM_DIM)


def gmlp_sample(u, vg, ws, bias):
    db, t = u.shape[:2]
    w = jnp.where(gmlp_mask()[:t, :t][None], ws[:, :t, :t], 0)
    mixed = jnp.einsum('gij,bjgc->bigc', w, vg.reshape(db, t, GM_GROUPS, GM_GDIM)) + bias.T[:t, :, None]
    return u * mixed.reshape(db, t, GM_DIM)


def peer(h, pk_wq, pk_keys, peer_u, peer_v):
    lead = h.shape[:-1]
    xt = h.reshape(-1, D_MODEL)
    n = xt.shape[0]
    nblk = -(-n // PEER_BLOCK)
    xt = jnp.pad(xt, ((0, nblk * PEER_BLOCK - n), (0, 0)))

    def retrieve(xb):
        q = (xb @ pk_wq).reshape(PEER_BLOCK, PK_HEADS, 2, PK_HALF)
        s = jnp.einsum('thpd,hpnd->thpn', q, pk_keys).astype(jnp.float32)
        sv, si = lax.top_k(s, PK_TOPK)
        cand = (sv[:, :, 0, :, None] + sv[:, :, 1, None, :]).reshape(PEER_BLOCK, PK_HEADS, PK_TOPK * PK_TOPK)
        cv, ci = lax.top_k(cand, PK_TOPK)
        i1 = jnp.take_along_axis(si[:, :, 0], ci // PK_TOPK, axis=-1)
        i2 = jnp.take_along_axis(si[:, :, 1], ci % PK_TOPK, axis=-1)
        eid = i1 * N_KEYS + i2
        g = jax.nn.softmax(cv, axis=-1)
        a = jax.nn.gelu(jnp.einsum('thkd,td->thk', peer_u[eid], xb)).astype(jnp.float32)
        return jnp.einsum('thk,thkd->td', (g * a).astype(xb.dtype), peer_v[eid])

    y = lax.map(retrieve, xt.reshape(nblk, PEER_BLOCK, D_MODEL))
    return y.reshape(nblk * PEER_BLOCK, D_MODEL)[:n].reshape(*lead, D_MODEL)


def _layer(x, c, cache_k, cache_v, w_mod, b_mod, g_norm1, w_in, attn_sinks, gm_ln_g, gm_ln_b, gm_ws, gm_b,
           w_branch_a, w_branch_b, w_out, g_norm2, pk_wq, pk_keys, peer_u, peer_v):
    nb, t = x.shape[:2]
    mod = (jax.nn.silu(c) @ w_mod + b_mod)[:, None, :]
    sh1, sc1, gt1, sh2, sc2, gt2 = jnp.split(mod, 6, axis=-1)
    h = rms_norm(x, g_norm1) * (1 + sc1) + sh1
    q, k, v, gu, gv, ga, gb = jnp.split(h @ w_in, SPLITS, axis=-1)
    q = q.reshape(nb, t, N_HEADS, HEAD_DIM)
    k = k.reshape(nb, t, N_KV, HEAD_DIM)
    v = v.reshape(nb, t, N_KV, HEAD_DIM)
    u = jax.nn.gelu(gu)
    vg = layer_norm(jax.nn.gelu(gv), gm_ln_g, gm_ln_b)
    if cache_k is None:
        o_a = swa_prompt(q, k, v, attn_sinks)
        o_b = gmlp_prompt(u, vg, gm_ws, gm_b)
        new_state = (k[:, -WINDOW:], v[:, -WINDOW:])
    else:
        o_a = swa_sample(q, k, v, cache_k, cache_v, attn_sinks)
        o_b = gmlp_sample(u, vg, gm_ws, gm_b)
        new_state = (k, v, vg)
    mix = jax.nn.sigmoid(ga) * (o_a @ w_branch_a) + jax.nn.sigmoid(gb) * (o_b @ w_branch_b)
    x = x + gt1 * (mix @ w_out)
    h2 = rms_norm(x, g_norm2) * (1 + sc2) + sh2
    x = x + gt2 * peer(h2, pk_wq, pk_keys, peer_u, peer_v)
    return x, new_state


def setup_inputs(seed: int = 0) -> dict:
    key = jax.random.key(seed)
    ks = jax.random.split(key, 24)
    f32 = jnp.float32
    rows = min(WINDOW, PAST_LEN)
    nrm = lambda k_, shape, sc: jax.random.normal(k_, shape, f32) * sc
    return {
        "x_prompt": nrm(ks[0], (BATCH, SEQ, D_MODEL), 1.0),
        "x_sample": nrm(ks[1], (DEC_BATCH, DEC_SEQ, D_MODEL), 1.0),
        "cache_k": nrm(ks[2], (DEPTH, DEC_BATCH, rows, N_KV, HEAD_DIM), 1.0),
        "cache_v": nrm(ks[3], (DEPTH, DEC_BATCH, rows, N_KV, HEAD_DIM), 1.0),
        "c_prompt": nrm(ks[4], (BATCH, D_MODEL), 1.0),
        "c_sample": nrm(ks[5], (DEC_BATCH, D_MODEL), 1.0),
        "w_mod": nrm(ks[6], (DEPTH, D_MODEL, 6 * D_MODEL), D_MODEL ** -0.5),
        "b_mod": nrm(ks[7], (DEPTH, 6 * D_MODEL), 0.01),
        "g_norm1": 1.0 + nrm(ks[8], (DEPTH, D_MODEL), 0.01),
        "w_in": nrm(ks[9], (DEPTH, D_MODEL, IN_DIM), D_MODEL ** -0.5),
        "attn_sinks": nrm(ks[10], (DEPTH, N_HEADS), 1.0),
        "gm_ln_g": 1.0 + nrm(ks[11], (DEPTH, GM_DIM), 0.01),
        "gm_ln_b": nrm(ks[12], (DEPTH, GM_DIM), 0.01),
        "gm_ws": nrm(ks[13], (DEPTH, GM_GROUPS, GM_BLOCK, GM_BLOCK), GM_BLOCK ** -0.5),
        "gm_b": 1.0 + nrm(ks[14], (DEPTH, GM_GROUPS, GM_BLOCK), 0.01),
        "w_branch_a": nrm(ks[15], (DEPTH, ATT_DIM, D_MODEL), ATT_DIM ** -0.5),
        "w_branch_b": nrm(ks[16], (DEPTH, GM_DIM, D_MODEL), GM_DIM ** -0.5),
        "w_out": nrm(ks[17], (DEPTH, D_MODEL, D_MODEL), D_MODEL ** -0.5),
        "g_norm2": 1.0 + nrm(ks[18], (DEPTH, D_MODEL), 0.01),
        "pk_wq": nrm(ks[19], (DEPTH, D_MODEL, PK_HEADS * PK_DIM), D_MODEL ** -0.5),
        "pk_keys": nrm(ks[20], (DEPTH, PK_HEADS, 2, N_KEYS, PK_HALF), PK_HALF ** -0.5),
        "peer_u": nrm(ks[21], (DEPTH, N_EXPERTS, D_MODEL), D_MODEL ** -0.5),
        "peer_v": nrm(ks[22], (DEPTH, N_EXPERTS, D_MODEL), (PK_HEADS * PK_TOPK) ** -0.5),
        "g_final": 1.0 + nrm(ks[23], (D_MODEL,), 0.01),
    }


def reference(x_prompt, x_sample, cache_k, cache_v, c_prompt, c_sample, w_mod, b_mod, g_norm1, w_in,
              attn_sinks, gm_ln_g, gm_ln_b, gm_ws, gm_b, w_branch_a, w_branch_b, w_out, g_norm2,
              pk_wq, pk_keys, peer_u, peer_v, g_final):
    xp, xs = x_prompt, x_sample
    kp, vp, ksm, vsm, gsm = [], [], [], [], []
    for l in range(DEPTH):
        w = (w_mod[l], b_mod[l], g_norm1[l], w_in[l], attn_sinks[l], gm_ln_g[l], gm_ln_b[l], gm_ws[l], gm_b[l],
             w_branch_a[l], w_branch_b[l], w_out[l], g_norm2[l], pk_wq[l], pk_keys[l], peer_u[l], peer_v[l])
        xp, (k_p, v_p) = _layer(xp, c_prompt, None, None, *w)
        xs, (k_s, v_s, g_s) = _layer(xs, c_sample, cache_k[l], cache_v[l], *w)
        kp.append(k_p)
        vp.append(v_p)
        ksm.append(k_s)
        vsm.append(v_s)
        gsm.append(g_s)
    y_prompt = rms_norm(xp, g_final)
    y_sample = rms_norm(xs, g_final)
    return (y_prompt, y_sample, jnp.stack(kp), jnp.stack(vp), jnp.stack(ksm), jnp.stack(vsm), jnp.stack(gsm))
```

```python
import functools

import jax
import jax.numpy as jnp
from jax import lax
from jax.experimental import pallas as pl
from jax.experimental.pallas import tpu as pltpu

F32 = jnp.float32
BF16 = jnp.bfloat16
I32 = jnp.int32

CHUNK = 64
N_HEADS = 16
N_KV = 4
GQA = N_HEADS // N_KV
HEAD_DIM = 64
ATT_DIM = N_HEADS * HEAD_DIM
KV_DIM = N_KV * HEAD_DIM
WINDOW = 128
WIN_CHUNKS = WINDOW // CHUNK
GM_BLOCK = 128
GM_DIM = 1024
GM_GROUPS = 8
GM_GDIM = GM_DIM // GM_GROUPS
N_KEYS = 128
PK_HEADS = 8
PK_HALF = 128
PK_TOPK = 16
EPS = 1e-6
NEG = -1e30

OFF_Q = 0
OFF_K = ATT_DIM
OFF_V = ATT_DIM + KV_DIM
OFF_GU = ATT_DIM + 2 * KV_DIM
OFF_GV = OFF_GU + GM_DIM
OFF_GA = OFF_GV + GM_DIM

V7X_VMEM_LIMIT_BYTES = 56 * 1024 * 1024

NT_DIMS = (((1,), (1,)), ((), ()))
TN_DIMS = (((0,), (0,)), ((), ()))

_CAND_GROUPS = ((0, 16, 16),) + tuple((k1, 8, PK_TOPK // (k1 + 1)) for k1 in range(1, 8))
_CAND_ROWS = sum(g[1] for g in _CAND_GROUPS) + 8


def _params(sem):
    return pltpu.CompilerParams(dimension_semantics=sem, vmem_limit_bytes=V7X_VMEM_LIMIT_BYTES)


def _mod_kernel(c_ref, w_ref, b_ref, o_ref):
    a = jax.nn.silu(c_ref[...])
    o_ref[...] = jnp.dot(a, w_ref[...], preferred_element_type=F32,
                         precision=lax.Precision.HIGHEST) + b_ref[...]


def _modulation(c, w_mod, b_mod, tn=1024):
    n, d = c.shape
    d6 = w_mod.shape[1]
    return pl.pallas_call(
        _mod_kernel,
        grid=(d6 // tn,),
        in_specs=[pl.BlockSpec((n, d), lambda j: (0, 0)),
                  pl.BlockSpec((d, tn), lambda j: (0, j)),
                  pl.BlockSpec((1, tn), lambda j: (0, j))],
        out_specs=pl.BlockSpec((n, tn), lambda j: (0, j)),
        out_shape=jax.ShapeDtypeStruct((n, d6), F32),
        compiler_params=_params(("parallel",)),
        name="modulation",
    )(c, w_mod, b_mod.reshape(1, d6))


def _rms_mod(x, g, sc, sh):
    y = x * lax.rsqrt(jnp.mean(x * x, axis=-1, keepdims=True) + EPS) * g
    return y * (1.0 + sc) + sh


def _inproj_kernel(x_ref, sc_ref, sh_ref, g_ref, w_ref, o_ref, h_ref):
    nb, r, d = x_ref.shape

    @pl.when(pl.program_id(1) == 0)
    def _():
        h = _rms_mod(x_ref[...], g_ref[...], sc_ref[...], sh_ref[...])
        h_ref[...] = h.reshape(nb * r, d).astype(BF16)

    o = jnp.dot(h_ref[...], w_ref[...], preferred_element_type=F32)
    o_ref[...] = o.reshape(o_ref.shape)


def _inproj(x, sc, sh, g, w, nb, r, tn=1280):
    b, s, d = x.shape
    n = w.shape[1]
    return pl.pallas_call(
        _inproj_kernel,
        grid=((b // nb) * (s // r), n // tn),
        in_specs=[pl.BlockSpec((nb, r, d), lambda i, j: (i // (s // r), i % (s // r), 0)),
                  pl.BlockSpec((nb, 1, d), lambda i, j: (i // (s // r), 0, 0)),
                  pl.BlockSpec((nb, 1, d), lambda i, j: (i // (s // r), 0, 0)),
                  pl.BlockSpec((1, 1, d), lambda i, j: (0, 0, 0)),
                  pl.BlockSpec((d, tn), lambda i, j: (0, j))],
        out_specs=pl.BlockSpec((nb, r, tn), lambda i, j: (i // (s // r), i % (s // r), j)),
        out_shape=jax.ShapeDtypeStruct((b, s, n), F32),
        scratch_shapes=[pltpu.VMEM((nb * r, d), BF16)],
        compiler_params=_params(("parallel", "arbitrary")),
        name="inproj",
    )(x, sc, sh, g.reshape(1, 1, d), w)


def _attn_kernel(q_ref, kp_ref, kc_ref, vp_ref, vc_ref, sink_ref, o_ref, *, band):
    tq = q_ref.shape[1]
    tp = kp_ref.shape[1]
    tk = tp + tq
    q = (q_ref[0] * (HEAD_DIM ** -0.5)).astype(BF16)
    k = jnp.concatenate([kp_ref[0], kc_ref[0]], axis=0).astype(BF16)
    v = jnp.concatenate([vp_ref[0], vc_ref[0]], axis=0).astype(BF16)
    if band:
        qc = lax.broadcasted_iota(I32, (tq, tk), 0) // CHUNK + tp // CHUNK
        kc = lax.broadcasted_iota(I32, (tq, tk), 1) // CHUNK
        mask = (kc <= qc) & (kc >= qc - WIN_CHUNKS)
        mask = mask & ((kc >= tp // CHUNK) | (pl.program_id(0) > 0))
    for g in range(N_KV):
        kg = k[:, g * HEAD_DIM:(g + 1) * HEAD_DIM]
        vg = v[:, g * HEAD_DIM:(g + 1) * HEAD_DIM]
        for hh in range(GQA):
            h = g * GQA + hh
            qh = q[:, h * HEAD_DIM:(h + 1) * HEAD_DIM]
            s = lax.dot_general(qh, kg, NT_DIMS, preferred_element_type=F32)
            if band:
                s = jnp.where(mask, s, NEG)
            sk = sink_ref[0, h]
            m = jnp.maximum(jnp.max(s, axis=-1, keepdims=True), sk)
            p = jnp.exp(s - m)
            den = jnp.sum(p, axis=-1, keepdims=True) + jnp.exp(sk - m)
            o = jnp.dot(p.astype(BF16), vg, preferred_element_type=F32) / den
            o_ref[0, :, h * HEAD_DIM:(h + 1) * HEAD_DIM] = o.astype(o_ref.dtype)


def _attn_prompt(z, sinks, tq=256):
    b, s, _ = z.shape
    tp = WINDOW
    kblk, vblk = OFF_K // KV_DIM, OFF_V // KV_DIM
    ratio = tq // tp

    def prev(i):
        return jnp.maximum(i * ratio - 1, 0)

    return pl.pallas_call(
        functools.partial(_attn_kernel, band=True),
        grid=(s // tq,),
        in_specs=[pl.BlockSpec((1, tq, ATT_DIM), lambda i: (0, i, 0)),
                  pl.BlockSpec((1, tp, KV_DIM), lambda i: (0, prev(i), kblk)),
                  pl.BlockSpec((1, tq, KV_DIM), lambda i: (0, i, kblk)),
                  pl.BlockSpec((1, tp, KV_DIM), lambda i: (0, prev(i), vblk)),
                  pl.BlockSpec((1, tq, KV_DIM), lambda i: (0, i, vblk)),
                  pl.BlockSpec(memory_space=pltpu.SMEM)],
        out_specs=pl.BlockSpec((1, tq, ATT_DIM), lambda i: (0, i, 0)),
        out_shape=jax.ShapeDtypeStruct((b, s, ATT_DIM), BF16),
        compiler_params=_params(("arbitrary",)),
        name="attn_prompt",
    )(z, z, z, z, z, sinks.reshape(1, N_HEADS))


def _attn_sample(z, ck, cv, sinks):
    b, t, _ = z.shape
    rows = ck.shape[1]
    kblk, vblk = OFF_K // KV_DIM, OFF_V // KV_DIM
    return pl.pallas_call(
        functools.partial(_attn_kernel, band=False),
        grid=(b,),
        in_specs=[pl.BlockSpec((1, t, ATT_DIM), lambda i: (i, 0, 0)),
                  pl.BlockSpec((1, rows, KV_DIM), lambda i: (i, 0, 0)),
                  pl.BlockSpec((1, t, KV_DIM), lambda i: (i, 0, kblk)),
                  pl.BlockSpec((1, rows, KV_DIM), lambda i: (i, 0, 0)),
                  pl.BlockSpec((1, t, KV_DIM), lambda i: (i, 0, vblk)),
                  pl.BlockSpec(memory_space=pltpu.SMEM)],
        out_specs=pl.BlockSpec((1, t, ATT_DIM), lambda i: (i, 0, 0)),
        out_shape=jax.ShapeDtypeStruct((b, t, ATT_DIM), BF16),
        compiler_params=_params(("arbitrary",)),
        name="attn_sample",
    )(z, ck, z, cv, z, sinks.reshape(1, N_HEADS))


def _gmlp_kernel(gu0_ref, gu1_ref, gv0_ref, gv1_ref, lg_ref, lb_ref, w_ref, b_ref, o_ref, *vg_out, rows):
    nblk = gu0_ref.shape[1] // rows
    gv = jax.nn.gelu(jnp.concatenate([gv0_ref[0], gv1_ref[0]], axis=-1))
    mu = jnp.mean(gv, axis=-1, keepdims=True)
    var = jnp.mean(jnp.square(gv - mu), axis=-1, keepdims=True)
    vg = (gv - mu) * lax.rsqrt(var + EPS) * lg_ref[...] + lb_ref[...]
    if vg_out:
        vg_out[0][0] = vg
    u = jax.nn.gelu(jnp.concatenate([gu0_ref[0], gu1_ref[0]], axis=-1))
    vgb = vg.astype(BF16)
    ii = lax.broadcasted_iota(I32, (rows, rows), 0) // CHUNK
    jj = lax.broadcasted_iota(I32, (rows, rows), 1) // CHUNK
    wmask = jj <= ii
    for g in range(GM_GROUPS):
        wg = jnp.where(wmask, w_ref[g], 0.0).astype(BF16)
        cols = slice(g * GM_GDIM, (g + 1) * GM_GDIM)
        for bi in range(nblk):
            rs = slice(bi * rows, (bi + 1) * rows)
            mixed = jnp.dot(wg, vgb[rs, cols], preferred_element_type=F32) + b_ref[g]
            o_ref[0, rs, cols] = (u[rs, cols] * mixed).astype(o_ref.dtype)


def _gmlp(z, ln_g, ln_b, ws, bias, rows, tr, emit_vg):
    b, s, _ = z.shape
    half = GM_DIM // 2
    ublk, vblk = OFF_GU // half, OFF_GV // half
    nt = s // tr

    def cols(blk):
        return pl.BlockSpec((1, tr, half), lambda i: (i // nt, i % nt, blk))

    out_shape = [jax.ShapeDtypeStruct((b, s, GM_DIM), BF16)]
    out_specs = [pl.BlockSpec((1, tr, GM_DIM), lambda i: (i // nt, i % nt, 0))]
    if emit_vg:
        out_shape.append(jax.ShapeDtypeStruct((b, s, GM_DIM), F32))
        out_specs.append(pl.BlockSpec((1, tr, GM_DIM), lambda i: (i // nt, i % nt, 0)))
    res = pl.pallas_call(
        functools.partial(_gmlp_kernel, rows=rows),
        grid=(b * nt,),
        in_specs=[cols(ublk), cols(ublk + 1), cols(vblk), cols(vblk + 1),
                  pl.BlockSpec((1, GM_DIM), lambda i: (0, 0)),
                  pl.BlockSpec((1, GM_DIM), lambda i: (0, 0)),
                  pl.BlockSpec((GM_GROUPS, rows, rows), lambda i: (0, 0, 0)),
                  pl.BlockSpec((GM_GROUPS, rows, 1), lambda i: (0, 0, 0))],
        out_specs=out_specs,
        out_shape=out_shape,
        compiler_params=_params(("parallel",)),
        name="gmlp",
    )(z, z, z, z, ln_g.reshape(1, GM_DIM), ln_b.reshape(1, GM_DIM), ws, bias.reshape(GM_GROUPS, rows, 1))
    return res if emit_vg else (res[0], None)


def _outproj_kernel(oa_ref, ob_ref, ga_ref, gb_ref, x_ref, gt_ref, sc_ref, sh_ref, g2_ref,
                    wa_ref, wb_ref, wo_ref, x1_ref, h2_ref, acc_ref):
    nb, r, d = x_ref.shape
    j = pl.program_id(1)
    tc = ga_ref.shape[2]
    oa = oa_ref[...].reshape(nb * r, oa_ref.shape[2])
    ob = ob_ref[...].reshape(nb * r, ob_ref.shape[2])
    t1 = jnp.dot(oa, wa_ref[...], preferred_element_type=F32)
    t2 = jnp.dot(ob, wb_ref[...], preferred_element_type=F32)
    ga = ga_ref[...].reshape(nb * r, tc)
    gb = gb_ref[...].reshape(nb * r, tc)
    mix = jax.nn.sigmoid(ga) * t1 + jax.nn.sigmoid(gb) * t2
    part = jnp.dot(mix.astype(BF16), wo_ref[...], preferred_element_type=F32)

    @pl.when(j == 0)
    def _():
        acc_ref[...] = part

    @pl.when(j > 0)
    def _():
        acc_ref[...] += part

    @pl.when(j == pl.num_programs(1) - 1)
    def _():
        x1 = x_ref[...] + gt_ref[...] * acc_ref[...].reshape(nb, r, d)
        x1_ref[...] = x1
        h2_ref[...] = _rms_mod(x1, g2_ref[...], sc_ref[...], sh_ref[...]).astype(h2_ref.dtype)


def _outproj(oa, ob, z, x, gt, sc, sh, g2, wa, wb, wo, nb, r, tc=512):
    b, s, d = x.shape
    nt = s // r
    ga0, gb0 = OFF_GA // tc, (OFF_GA + d) // tc

    def rows(i, j):
        return (i // nt, i % nt, 0)

    def per_stream(i, j):
        return (i // nt, 0, 0)

    return pl.pallas_call(
        _outproj_kernel,
        grid=((b // nb) * nt, d // tc),
        in_specs=[pl.BlockSpec((nb, r, ATT_DIM), rows),
                  pl.BlockSpec((nb, r, GM_DIM), rows),
                  pl.BlockSpec((nb, r, tc), lambda i, j: (i // nt, i % nt, ga0 + j)),
                  pl.BlockSpec((nb, r, tc), lambda i, j: (i // nt, i % nt, gb0 + j)),
                  pl.BlockSpec((nb, r, d), rows),
                  pl.BlockSpec((nb, 1, d), per_stream),
                  pl.BlockSpec((nb, 1, d), per_stream),
                  pl.BlockSpec((nb, 1, d), per_stream),
                  pl.BlockSpec((1, 1, d), lambda i, j: (0, 0, 0)),
                  pl.BlockSpec((ATT_DIM, tc), lambda i, j: (0, j)),
                  pl.BlockSpec((GM_DIM, tc), lambda i, j: (0, j)),
                  pl.BlockSpec((tc, d), lambda i, j: (j, 0))],
        out_specs=[pl.BlockSpec((nb, r, d), rows), pl.BlockSpec((nb, r, d), rows)],
        out_shape=[jax.ShapeDtypeStruct((b, s, d), F32), jax.ShapeDtypeStruct((b, s, d), BF16)],
        scratch_shapes=[pltpu.VMEM((nb * r, d), F32)],
        compiler_params=_params(("parallel", "arbitrary")),
        name="outproj",
    )(oa, ob, z, z, x, gt, sc, sh, g2.reshape(1, 1, d), wa, wb, wo)


def _peer_select_kernel(h_ref, wq_ref, keys_ref, c1_ref, f1_ref, r2_ref, f2_ref,
                        s_ref, work_ref, rank_ref, sv_ref, cand_ref, fid_ref):
    tm = h_ref.shape[0]
    neg_inf = jnp.float32(-jnp.inf)
    q = jnp.dot(h_ref[...], wq_ref[...], preferred_element_type=F32).astype(BF16)
    row = lax.broadcasted_iota(I32, (N_KEYS, tm), 0)

    for p in range(2):
        s = lax.dot_general(keys_ref[0, p], q[:, p * PK_HALF:(p + 1) * PK_HALF], NT_DIMS,
                            preferred_element_type=F32)
        s_ref[p] = s
        work_ref[...] = s
        rank_ref[p] = jnp.full((N_KEYS, tm), PK_TOPK, I32)

        def extract(k, carry, p=p):
            w = work_ref[...]
            m = jnp.max(w, axis=0, keepdims=True)
            idx = jnp.min(jnp.where(w == m, row, N_KEYS), axis=0, keepdims=True)
            hit = row == idx
            work_ref[...] = jnp.where(hit, neg_inf, w)
            rank_ref[p] = jnp.where(hit, k, rank_ref[p])
            sv_ref[p, pl.ds(k, 1), :] = m
            return carry

        lax.fori_loop(0, PK_TOPK, extract, 0)

    sv1 = sv_ref[0]
    sv2 = sv_ref[1]
    r8 = lax.broadcasted_iota(I32, (8, tm), 0)
    r16 = lax.broadcasted_iota(I32, (16, tm), 0)
    off = 0
    for k1, nrows, nused in _CAND_GROUPS:
        rr = r16 if nrows == 16 else r8
        val = sv1[k1:k1 + 1, :] + sv2[0:nrows, :]
        cand_ref[off:off + nrows, :] = jnp.where(rr < nused, val, neg_inf)
        fid_ref[off:off + nrows, :] = k1 * PK_TOPK + rr
        off += nrows
    cand_ref[off:off + 8, :] = sv1[8:16, :] + sv2[0:1, :]
    fid_ref[off:off + 8, :] = (r8 + 8) * PK_TOPK

    m0 = sv1[0:1, :] + sv2[0:1, :]
    big = jnp.int32(PK_TOPK * PK_TOPK)

    def extract2(k, carry):
        cnt, zsum = carry
        w = cand_ref[...]
        fid = fid_ref[...]
        m = jnp.max(w, axis=0, keepdims=True)
        idx = jnp.min(jnp.where(w == m, fid, big), axis=0, keepdims=True)
        cand_ref[...] = jnp.where(fid == idx, neg_inf, w)
        cnt = cnt + (r16 == lax.shift_right_logical(idx, 4)).astype(I32)
        zsum = zsum + jnp.exp(m - m0)
        return cnt, zsum

    cnt, zsum = lax.fori_loop(0, PK_TOPK, extract2,
                              (jnp.zeros((PK_TOPK, tm), I32), jnp.zeros((1, tm), F32)))

    rank1 = rank_ref[0]
    c1 = jnp.zeros((N_KEYS, tm), I32)
    for k1 in range(PK_TOPK):
        c1 = jnp.where(rank1 == k1, cnt[k1:k1 + 1, :], c1)
    c1_ref[0] = c1
    f1_ref[0] = jnp.exp(s_ref[0] - sv1[0:1, :])
    r2_ref[0] = rank_ref[1]
    f2_ref[0] = jnp.exp(s_ref[1] - sv2[0:1, :]) / zsum


def _peer_select(h2, wq, keys, tm):
    t, d = h2.shape
    hw = 2 * PK_HALF
    sel = jax.ShapeDtypeStruct((PK_HEADS, N_KEYS, t), F32)
    seli = jax.ShapeDtypeStruct((PK_HEADS, N_KEYS, t), I32)
    ospec = pl.BlockSpec((1, N_KEYS, tm), lambda i, h: (h, 0, i))
    return pl.pallas_call(
        _peer_select_kernel,
        grid=(t // tm, PK_HEADS),
        in_specs=[pl.BlockSpec((tm, d), lambda i, h: (i, 0)),
                  pl.BlockSpec((d, hw), lambda i, h: (0, h)),
                  pl.BlockSpec((1, 2, N_KEYS, PK_HALF), lambda i, h: (h, 0, 0, 0))],
        out_specs=[ospec, ospec, ospec, ospec],
        out_shape=[seli, sel, seli, sel],
        scratch_shapes=[pltpu.VMEM((2, N_KEYS, tm), F32),
                        pltpu.VMEM((N_KEYS, tm), F32),
                        pltpu.VMEM((2, N_KEYS, tm), I32),
                        pltpu.VMEM((2, PK_TOPK, tm), F32),
                        pltpu.VMEM((_CAND_ROWS, tm), F32),
                        pltpu.VMEM((_CAND_ROWS, tm), I32)],
        compiler_params=_params(("parallel", "arbitrary")),
        name="peer_select",
    )(h2, wq, keys)


def _peer_dense_kernel(h_ref, c1_ref, f1_ref, r2_ref, f2_ref, u_ref, v_ref, y_ref, w_ref):
    j = pl.program_id(1)
    te = u_ref.shape[0]
    tm = h_ref.shape[0]
    for a in range(te // N_KEYS):
        i1 = j * (te // N_KEYS) + a
        w = jnp.zeros((N_KEYS, tm), F32)
        for h in range(PK_HEADS):
            c1 = c1_ref[h, pl.ds(i1, 1), :]
            f1 = f1_ref[h, pl.ds(i1, 1), :]
            w = w + jnp.where(r2_ref[h] < c1, f2_ref[h] * f1, 0.0)
        w_ref[a * N_KEYS:(a + 1) * N_KEYS, :] = w
    act = lax.dot_general(u_ref[...], h_ref[...], NT_DIMS, preferred_element_type=F32)
    gated = (jax.nn.gelu(act) * w_ref[...]).astype(BF16)
    part = lax.dot_general(gated, v_ref[...], TN_DIMS, preferred_element_type=F32)

    @pl.when(j == 0)
    def _():
        y_ref[...] = part

    @pl.when(j > 0)
    def _():
        y_ref[...] += part


def _peer_dense(h2, sel, pu, pv, tm, te):
    t, d = h2.shape
    ne = pu.shape[0]
    sspec = pl.BlockSpec((PK_HEADS, N_KEYS, tm), lambda i, j: (0, 0, i))
    return pl.pallas_call(
        _peer_dense_kernel,
        grid=(t // tm, ne // te),
        in_specs=[pl.BlockSpec((tm, d), lambda i, j: (i, 0)),
                  sspec, sspec, sspec, sspec,
                  pl.BlockSpec((te, d), lambda i, j: (j, 0)),
                  pl.BlockSpec((te, d), lambda i, j: (j, 0))],
        out_specs=pl.BlockSpec((tm, d), lambda i, j: (i, 0)),
        out_shape=jax.ShapeDtypeStruct((t, d), F32),
        scratch_shapes=[pltpu.VMEM((te, tm), F32)],
        compiler_params=_params(("parallel", "arbitrary")),
        name="peer_dense",
    )(h2, *sel, pu, pv)


def _residual_kernel(x_ref, y_ref, gt_ref, g_ref, o_ref, *, final):
    x2 = x_ref[...] + gt_ref[...] * y_ref[...].reshape(x_ref.shape)
    if final:
        x2 = x2 * lax.rsqrt(jnp.mean(x2 * x2, axis=-1, keepdims=True) + EPS) * g_ref[...]
    o_ref[...] = x2


def _residual(x1, y, gt, g_final, nb, r, final):
    b, s, d = x1.shape
    nt = s // r
    return pl.pallas_call(
        functools.partial(_residual_kernel, final=final),
        grid=((b // nb) * nt,),
        in_specs=[pl.BlockSpec((nb, r, d), lambda i: (i // nt, i % nt, 0)),
                  pl.BlockSpec((nb * r, d), lambda i: (i, 0)),
                  pl.BlockSpec((nb, 1, d), lambda i: (i // nt, 0, 0)),
                  pl.BlockSpec((1, 1, d), lambda i: (0, 0, 0))],
        out_specs=pl.BlockSpec((nb, r, d), lambda i: (i // nt, i % nt, 0)),
        out_shape=jax.ShapeDtypeStruct((b, s, d), F32),
        compiler_params=_params(("parallel",)),
        name="residual",
    )(x1, y, gt, g_final.reshape(1, 1, d))


def _layer(x, mods, cache, wts, g_final, final, nb, r, gm_rows, gm_tr, peer_tm, peer_te):
    (g1, w_in, sinks, ln_g, ln_b, ws, gm_b, wa, wb, wo, g2, wq, keys, pu, pv) = wts
    sh1, sc1, gt1, sh2, sc2, gt2 = mods
    b, s, d = x.shape
    z = _inproj(x, sc1, sh1, g1, w_in, nb, r)
    k = z[:, :, OFF_K:OFF_K + KV_DIM].reshape(b, s, N_KV, HEAD_DIM)
    v = z[:, :, OFF_V:OFF_V + KV_DIM].reshape(b, s, N_KV, HEAD_DIM)
    if cache is None:
        o_a = _attn_prompt(z, sinks)
        o_b, _ = _gmlp(z, ln_g, ln_b, ws, gm_b, gm_rows, gm_tr, emit_vg=False)
        state = (k[:, -WINDOW:], v[:, -WINDOW:])
    else:
        ck, cv = cache
        o_a = _attn_sample(z, ck.reshape(b, ck.shape[1], KV_DIM), cv.reshape(b, cv.shape[1], KV_DIM), sinks)
        o_b, vg = _gmlp(z, ln_g, ln_b, ws[:, :gm_rows, :gm_rows], gm_b[:, :gm_rows], gm_rows, gm_tr,
                        emit_vg=True)
        state = (k, v, vg)
    x1, h2 = _outproj(o_a, o_b, z, x, gt1, sc2, sh2, g2, wa, wb, wo, nb, r)
    h2f = h2.reshape(b * s, d)
    sel = _peer_select(h2f, wq, keys, peer_tm)
    y = _peer_dense(h2f, sel, pu, pv, peer_tm, peer_te)
    out = _residual(x1, y, gt2, g_final, nb, r, final)
    return out, state


def kernel(x_prompt, x_sample, cache_k, cache_v, c_prompt, c_sample, w_mod, b_mod, g_norm1, w_in, attn_sinks, gm_ln_g, gm_ln_b, gm_ws, gm_b, w_branch_a, w_branch_b, w_out, g_norm2, pk_wq, pk_keys, peer_u, peer_v, g_final):
    depth = w_mod.shape[0]
    bp, sp, d = x_prompt.shape
    bs, ts, _ = x_sample.shape
    assert bp == 1 and sp % 512 == 0 and (bs * ts) % 512 == 0 and ts <= CHUNK and ts % 8 == 0
    xp, xs = x_prompt, x_sample
    nc = bp + bs
    npad = -(-nc // 8) * 8
    kp, vp, ksm, vsm, gsm = [], [], [], [], []
    for l in range(depth):
        c_all = jnp.pad(jnp.concatenate([c_prompt, c_sample], axis=0), ((0, npad - nc), (0, 0)))
        mod = _modulation(c_all, w_mod[l], b_mod[l])
        mods_p = tuple(m.reshape(bp, 1, d) for m in jnp.split(mod[:bp], 6, axis=-1))
        mods_s = tuple(m.reshape(bs, 1, d) for m in jnp.split(mod[bp:nc], 6, axis=-1))
        wts = (g_norm1[l], w_in[l].astype(BF16), attn_sinks[l], gm_ln_g[l], gm_ln_b[l], gm_ws[l], gm_b[l],
               w_branch_a[l].astype(BF16), w_branch_b[l].astype(BF16), w_out[l].astype(BF16), g_norm2[l],
               pk_wq[l].astype(BF16), pk_keys[l].astype(BF16), peer_u[l].astype(BF16), peer_v[l].astype(BF16))
        final = l == depth - 1
        xp, (k_p, v_p) = _layer(xp, mods_p, None, wts, g_final, final, nb=1, r=512,
                                gm_rows=GM_BLOCK, gm_tr=512, peer_tm=512, peer_te=512)
        xs, (k_s, v_s, g_s) = _layer(xs, mods_s, (cache_k[l], cache_v[l]), wts, g_final, final,
                                     nb=512 // ts, r=ts, gm_rows=ts, gm_tr=ts, peer_tm=512, peer_te=512)
        kp.append(k_p)
        vp.append(v_p)
        ksm.append(k_s)
        vsm.append(v_s)
        gsm.append(g_s)
    return (xp, xs, jnp.stack(kp), jnp.stack(vp), jnp.stack(ksm), jnp.stack(vsm), jnp.stack(gsm))
```

```python
import functools

import jax
import jax.numpy as jnp
from jax import lax
from jax.experimental import pallas as pl
from jax.experimental.pallas import tpu as pltpu

F32 = jnp.float32
BF16 = jnp.bfloat16
I32 = jnp.int32

CHUNK = 64
N_HEADS = 16
N_KV = 4
GQA = N_HEADS // N_KV
HEAD_DIM = 64
ATT_DIM = N_HEADS * HEAD_DIM
KV_DIM = N_KV * HEAD_DIM
WINDOW = 128
WIN_CHUNKS = WINDOW // CHUNK
GM_BLOCK = 128
GM_DIM = 1024
GM_GROUPS = 8
GM_GDIM = GM_DIM // GM_GROUPS
N_KEYS = 128
PK_HEADS = 8
PK_HALF = 128
PK_TOPK = 16
EPS = 1e-6
NEG = -1e30

OFF_Q = 0
OFF_K = ATT_DIM
OFF_V = ATT_DIM + KV_DIM
OFF_GU = ATT_DIM + 2 * KV_DIM
OFF_GV = OFF_GU + GM_DIM
OFF_GA = OFF_GV + GM_DIM

V7X_VMEM_LIMIT_BYTES = 56 * 1024 * 1024

NT_DIMS = (((1,), (1,)), ((), ()))
TN_DIMS = (((0,), (0,)), ((), ()))

_CAND_GROUPS = ((0, 16, 16),) + tuple((k1, 8, PK_TOPK // (k1 + 1)) for k1 in range(1, 8))
_CAND_ROWS = sum(g[1] for g in _CAND_GROUPS) + 8


def _params(sem):
    return pltpu.CompilerParams(dimension_semantics=sem, vmem_limit_bytes=V7X_VMEM_LIMIT_BYTES)


def _mod_kernel(c_ref, w_ref, b_ref, o_ref):
    a = jax.nn.silu(c_ref[...])
    o_ref[...] = jnp.dot(a, w_ref[...], preferred_element_type=F32,
                         precision=lax.Precision.HIGHEST) + b_ref[...]


def _modulation(c, w_mod, b_mod, tn=1024):
    n, d = c.shape
    d6 = w_mod.shape[1]
    return pl.pallas_call(
        _mod_kernel,
        grid=(d6 // tn,),
        in_specs=[pl.BlockSpec((n, d), lambda j: (0, 0)),
                  pl.BlockSpec((d, tn), lambda j: (0, j)),
                  pl.BlockSpec((1, tn), lambda j: (0, j))],
        out_specs=pl.BlockSpec((n, tn), lambda j: (0, j)),
        out_shape=jax.ShapeDtypeStruct((n, d6), F32),
        compiler_params=_params(("parallel",)),
        name="modulation",
    )(c, w_mod, b_mod.reshape(1, d6))


def _rms_mod(x, g, sc, sh):
    y = x * lax.rsqrt(jnp.mean(x * x, axis=-1, keepdims=True) + EPS) * g
    return y * (1.0 + sc) + sh


def _inproj_kernel(x_ref, sc_ref, sh_ref, g_ref, w_ref, o_ref, h_ref):
    nb, r, d = x_ref.shape

    @pl.when(pl.program_id(1) == 0)
    def _():
        h = _rms_mod(x_ref[...], g_ref[...], sc_ref[...], sh_ref[...])
        h_ref[...] = h.reshape(nb * r, d).astype(BF16)

    o = jnp.dot(h_ref[...], w_ref[...], preferred_element_type=F32)
    o_ref[...] = o.reshape(o_ref.shape)


def _inproj(x, sc, sh, g, w, nb, r, tn=1280):
    b, s, d = x.shape
    n = w.shape[1]
    return pl.pallas_call(
        _inproj_kernel,
        grid=((b // nb) * (s // r), n // tn),
        in_specs=[pl.BlockSpec((nb, r, d), lambda i, j: (i // (s // r), i % (s // r), 0)),
                  pl.BlockSpec((nb, 1, d), lambda i, j: (i // (s // r), 0, 0)),
                  pl.BlockSpec((nb, 1, d), lambda i, j: (i // (s // r), 0, 0)),
                  pl.BlockSpec((1, 1, d), lambda i, j: (0, 0, 0)),
                  pl.BlockSpec((d, tn), lambda i, j: (0, j))],
        out_specs=pl.BlockSpec((nb, r, tn), lambda i, j: (i // (s // r), i % (s // r), j)),
        out_shape=jax.ShapeDtypeStruct((b, s, n), F32),
        scratch_shapes=[pltpu.VMEM((nb * r, d), BF16)],
        compiler_params=_params(("parallel", "arbitrary")),
        name="inproj",
    )(x, sc, sh, g.reshape(1, 1, d), w)


def _attn_kernel(q_ref, kp_ref, kc_ref, vp_ref, vc_ref, sink_ref, o_ref, *, band):
    tq = q_ref.shape[1]
    tp = kp_ref.shape[1]
    tk = tp + tq
    q = (q_ref[0] * (HEAD_DIM ** -0.5)).astype(BF16)
    k = jnp.concatenate([kp_ref[0], kc_ref[0]], axis=0).astype(BF16)
    v = jnp.concatenate([vp_ref[0], vc_ref[0]], axis=0).astype(BF16)
    if band:
        qc = lax.broadcasted_iota(I32, (tq, tk), 0) // CHUNK + tp // CHUNK
        kc = lax.broadcasted_iota(I32, (tq, tk), 1) // CHUNK
        mask = (kc <= qc) & (kc >= qc - WIN_CHUNKS)
        mask = mask & ((kc >= tp // CHUNK) | (pl.program_id(0) > 0))
    for g in range(N_KV):
        kg = k[:, g * HEAD_DIM:(g + 1) * HEAD_DIM]
        vg = v[:, g * HEAD_DIM:(g + 1) * HEAD_DIM]
        for hh in range(GQA):
            h = g * GQA + hh
            qh = q[:, h * HEAD_DIM:(h + 1) * HEAD_DIM]
            s = lax.dot_general(qh, kg, NT_DIMS, preferred_element_type=F32)
            if band:
                s = jnp.where(mask, s, NEG)
            sk = sink_ref[0, h]
            m = jnp.maximum(jnp.max(s, axis=-1, keepdims=True), sk)
            p = jnp.exp(s - m)
            den = jnp.sum(p, axis=-1, keepdims=True) + jnp.exp(sk - m)
            o = jnp.dot(p.astype(BF16), vg, preferred_element_type=F32) / den
            o_ref[0, :, h * HEAD_DIM:(h + 1) * HEAD_DIM] = o.astype(o_ref.dtype)


def _attn_prompt(z, sinks, tq=256):
    b, s, _ = z.shape
    tp = WINDOW
    kblk, vblk = OFF_K // KV_DIM, OFF_V // KV_DIM
    ratio = tq // tp

    def prev(i):
        return jnp.maximum(i * ratio - 1, 0)

    return pl.pallas_call(
        functools.partial(_attn_kernel, band=True),
        grid=(s // tq,),
        in_specs=[pl.BlockSpec((1, tq, ATT_DIM), lambda i: (0, i, 0)),
                  pl.BlockSpec((1, tp, KV_DIM), lambda i: (0, prev(i), kblk)),
                  pl.BlockSpec((1, tq, KV_DIM), lambda i: (0, i, kblk)),
                  pl.BlockSpec((1, tp, KV_DIM), lambda i: (0, prev(i), vblk)),
                  pl.BlockSpec((1, tq, KV_DIM), lambda i: (0, i, vblk)),
                  pl.BlockSpec(memory_space=pltpu.SMEM)],
        out_specs=pl.BlockSpec((1, tq, ATT_DIM), lambda i: (0, i, 0)),
        out_shape=jax.ShapeDtypeStruct((b, s, ATT_DIM), BF16),
        compiler_params=_params(("arbitrary",)),
        name="attn_prompt",
    )(z, z, z, z, z, sinks.reshape(1, N_HEADS))


def _attn_sample(z, ck, cv, sinks):
    b, t, _ = z.shape
    rows = ck.shape[1]
    kblk, vblk = OFF_K // KV_DIM, OFF_V // KV_DIM
    return pl.pallas_call(
        functools.partial(_attn_kernel, band=False),
        grid=(b,),
        in_specs=[pl.BlockSpec((1, t, ATT_DIM), lambda i: (i, 0, 0)),
                  pl.BlockSpec((1, rows, KV_DIM), lambda i: (i, 0, 0)),
                  pl.BlockSpec((1, t, KV_DIM), lambda i: (i, 0, kblk)),
                  pl.BlockSpec((1, rows, KV_DIM), lambda i: (i, 0, 0)),
                  pl.BlockSpec((1, t, KV_DIM), lambda i: (i, 0, vblk)),
                  pl.BlockSpec(memory_space=pltpu.SMEM)],
        out_specs=pl.BlockSpec((1, t, ATT_DIM), lambda i: (i, 0, 0)),
        out_shape=jax.ShapeDtypeStruct((b, t, ATT_DIM), BF16),
        compiler_params=_params(("arbitrary",)),
        name="attn_sample",
    )(z, ck, z, cv, z, sinks.reshape(1, N_HEADS))


def _gmlp_kernel(gu0_ref, gu1_ref, gv0_ref, gv1_ref, lg_ref, lb_ref, w_ref, b_ref, o_ref, *vg_out, rows):
    nblk = gu0_ref.shape[1] // rows
    gv = jax.nn.gelu(jnp.concatenate([gv0_ref[0], gv1_ref[0]], axis=-1))
    mu = jnp.mean(gv, axis=-1, keepdims=True)
    var = jnp.mean(jnp.square(gv - mu), axis=-1, keepdims=True)
    vg = (gv - mu) * lax.rsqrt(var + EPS) * lg_ref[...] + lb_ref[...]
    if vg_out:
        vg_out[0][0] = vg
    u = jax.nn.gelu(jnp.concatenate([gu0_ref[0], gu1_ref[0]], axis=-1))
    vgb = vg.astype(BF16)
    ii = lax.broadcasted_iota(I32, (rows, rows), 0) // CHUNK
    jj = lax.broadcasted_iota(I32, (rows, rows), 1) // CHUNK
    wmask = jj <= ii
    for g in range(GM_GROUPS):
        wg = jnp.where(wmask, w_ref[g], 0.0).astype(BF16)
        cols = slice(g * GM_GDIM, (g + 1) * GM_GDIM)
        for bi in range(nblk):
            rs = slice(bi * rows, (bi + 1) * rows)
            mixed = jnp.dot(wg, vgb[rs, cols], preferred_element_type=F32) + b_ref[g]
            o_ref[0, rs, cols] = (u[rs, cols] * mixed).astype(o_ref.dtype)


def _gmlp(z, ln_g, ln_b, ws, bias, rows, tr, emit_vg):
    b, s, _ = z.shape
    half = GM_DIM // 2
    ublk, vblk = OFF_GU // half, OFF_GV // half
    nt = s // tr

    def cols(blk):
        return pl.BlockSpec((1, tr, half), lambda i: (i // nt, i % nt, blk))

    out_shape = [jax.ShapeDtypeStruct((b, s, GM_DIM), BF16)]
    out_specs = [pl.BlockSpec((1, tr, GM_DIM), lambda i: (i // nt, i % nt, 0))]
    if emit_vg:
        out_shape.append(jax.ShapeDtypeStruct((b, s, GM_DIM), F32))
        out_specs.append(pl.BlockSpec((1, tr, GM_DIM), lambda i: (i // nt, i % nt, 0)))
    res = pl.pallas_call(
        functools.partial(_gmlp_kernel, rows=rows),
        grid=(b * nt,),
        in_specs=[cols(ublk), cols(ublk + 1), cols(vblk), cols(vblk + 1),
                  pl.BlockSpec((1, GM_DIM), lambda i: (0, 0)),
                  pl.BlockSpec((1, GM_DIM), lambda i: (0, 0)),
                  pl.BlockSpec((GM_GROUPS, rows, rows), lambda i: (0, 0, 0)),
                  pl.BlockSpec((GM_GROUPS, rows, 1), lambda i: (0, 0, 0))],
        out_specs=out_specs,
        out_shape=out_shape,
        compiler_params=_params(("parallel",)),
        name="gmlp",
    )(z, z, z, z, ln_g.reshape(1, GM_DIM), ln_b.reshape(1, GM_DIM), ws, bias.reshape(GM_GROUPS, rows, 1))
    return res if emit_vg else (res[0], None)


def _outproj_kernel(oa_ref, ob_ref, ga_ref, gb_ref, x_ref, gt_ref, sc_ref, sh_ref, g2_ref,
                    wa_ref, wb_ref, wo_ref, x1_ref, h2_ref, acc_ref):
    nb, r, d = x_ref.shape
    j = pl.program_id(1)
    tc = ga_ref.shape[2]
    oa = oa_ref[...].reshape(nb * r, oa_ref.shape[2])
    ob = ob_ref[...].reshape(nb * r, ob_ref.shape[2])
    t1 = jnp.dot(oa, wa_ref[...], preferred_element_type=F32)
    t2 = jnp.dot(ob, wb_ref[...], preferred_element_type=F32)
    ga = ga_ref[...].reshape(nb * r, tc)
    gb = gb_ref[...].reshape(nb * r, tc)
    mix = jax.nn.sigmoid(ga) * t1 + jax.nn.sigmoid(gb) * t2
    part = jnp.dot(mix.astype(BF16), wo_ref[...], preferred_element_type=F32)

    @pl.when(j == 0)
    def _():
        acc_ref[...] = part

    @pl.when(j > 0)
    def _():
        acc_ref[...] += part

    @pl.when(j == pl.num_programs(1) - 1)
    def _():
        x1 = x_ref[...] + gt_ref[...] * acc_ref[...].reshape(nb, r, d)
        x1_ref[...] = x1
        h2_ref[...] = _rms_mod(x1, g2_ref[...], sc_ref[...], sh_ref[...]).astype(h2_ref.dtype)


def _outproj(oa, ob, z, x, gt, sc, sh, g2, wa, wb, wo, nb, r, tc=512):
    b, s, d = x.shape
    nt = s // r
    ga0, gb0 = OFF_GA // tc, (OFF_GA + d) // tc

    def rows(i, j):
        return (i // nt, i % nt, 0)

    def per_stream(i, j):
        return (i // nt, 0, 0)

    return pl.pallas_call(
        _outproj_kernel,
        grid=((b // nb) * nt, d // tc),
        in_specs=[pl.BlockSpec((nb, r, ATT_DIM), rows),
                  pl.BlockSpec((nb, r, GM_DIM), rows),
                  pl.BlockSpec((nb, r, tc), lambda i, j: (i // nt, i % nt, ga0 + j)),
                  pl.BlockSpec((nb, r, tc), lambda i, j: (i // nt, i % nt, gb0 + j)),
                  pl.BlockSpec((nb, r, d), rows),
                  pl.BlockSpec((nb, 1, d), per_stream),
                  pl.BlockSpec((nb, 1, d), per_stream),
                  pl.BlockSpec((nb, 1, d), per_stream),
                  pl.BlockSpec((1, 1, d), lambda i, j: (0, 0, 0)),
                  pl.BlockSpec((ATT_DIM, tc), lambda i, j: (0, j)),
                  pl.BlockSpec((GM_DIM, tc), lambda i, j: (0, j)),
                  pl.BlockSpec((tc, d), lambda i, j: (j, 0))],
        out_specs=[pl.BlockSpec((nb, r, d), rows), pl.BlockSpec((nb, r, d), rows)],
        out_shape=[jax.ShapeDtypeStruct((b, s, d), F32), jax.ShapeDtypeStruct((b, s, d), BF16)],
        scratch_shapes=[pltpu.VMEM((nb * r, d), F32)],
        compiler_params=_params(("parallel", "arbitrary")),
        name="outproj",
    )(oa, ob, z, z, x, gt, sc, sh, g2.reshape(1, 1, d), wa, wb, wo)


def _peer_select_kernel(h_ref, wq_ref, keys_ref, c1_ref, f1_ref, r2_ref, f2_ref,
                        s_ref, work_ref, rank_ref, sv_ref, cand_ref, fid_ref):
    tm = h_ref.shape[0]
    neg_inf = jnp.float32(-jnp.inf)
    q = jnp.dot(h_ref[...], wq_ref[...], preferred_element_type=F32).astype(BF16)
    row = lax.broadcasted_iota(I32, (N_KEYS, tm), 0)

    for p in range(2):
        s = lax.dot_general(keys_ref[0, p], q[:, p * PK_HALF:(p + 1) * PK_HALF], NT_DIMS,
                            preferred_element_type=F32)
        s_ref[p] = s
        work_ref[...] = s
        rank_ref[p] = jnp.full((N_KEYS, tm), PK_TOPK, I32)

        def extract(k, carry, p=p):
            w = work_ref[...]
            m = jnp.max(w, axis=0, keepdims=True)
            idx = jnp.min(jnp.where(w == m, row, N_KEYS), axis=0, keepdims=True)
            hit = row == idx
            work_ref[...] = jnp.where(hit, neg_inf, w)
            rank_ref[p] = jnp.where(hit, k, rank_ref[p])
            sv_ref[p, pl.ds(k, 1), :] = m
            return carry

        lax.fori_loop(0, PK_TOPK, extract, 0)

    sv1 = sv_ref[0]
    sv2 = sv_ref[1]
    r8 = lax.broadcasted_iota(I32, (8, tm), 0)
    r16 = lax.broadcasted_iota(I32, (16, tm), 0)
    off = 0
    for k1, nrows, nused in _CAND_GROUPS:
        rr = r16 if nrows == 16 else r8
        val = sv1[k1:k1 + 1, :] + sv2[0:nrows, :]
        cand_ref[off:off + nrows, :] = jnp.where(rr < nused, val, neg_inf)
        fid_ref[off:off + nrows, :] = k1 * PK_TOPK + rr
        off += nrows
    cand_ref[off:off + 8, :] = sv1[8:16, :] + sv2[0:1, :]
    fid_ref[off:off + 8, :] = (r8 + 8) * PK_TOPK

    m0 = sv1[0:1, :] + sv2[0:1, :]
    big = jnp.int32(PK_TOPK * PK_TOPK)

    def extract2(k, carry):
        cnt, zsum = carry
        w = cand_ref[...]
        fid = fid_ref[...]
        m = jnp.max(w, axis=0, keepdims=True)
        idx = jnp.min(jnp.where(w == m, fid, big), axis=0, keepdims=True)
        cand_ref[...] = jnp.where(fid == idx, neg_inf, w)
        cnt = cnt + (r16 == lax.shift_right_logical(idx, 4)).astype(I32)
        zsum = zsum + jnp.exp(m - m0)
        return cnt, zsum

    cnt, zsum = lax.fori_loop(0, PK_TOPK, extract2,
                              (jnp.zeros((PK_TOPK, tm), I32), jnp.zeros((1, tm), F32)))

    rank1 = rank_ref[0]
    c1 = jnp.zeros((N_KEYS, tm), I32)
    for k1 in range(PK_TOPK):
        c1 = jnp.where(rank1 == k1, cnt[k1:k1 + 1, :], c1)
    c1_ref[0] = c1.astype(c1_ref.dtype)
    f1_ref[0] = jnp.exp(s_ref[0] - sv1[0:1, :])
    r2_ref[0] = rank_ref[1].astype(r2_ref.dtype)
    f2_ref[0] = (jnp.exp(s_ref[1] - sv2[0:1, :]) / zsum).astype(f2_ref.dtype)


def _peer_select(h2, wq, keys, tm):
    t, d = h2.shape
    hw = 2 * PK_HALF
    sel = jax.ShapeDtypeStruct((PK_HEADS, N_KEYS, t), F32)
    selb = jax.ShapeDtypeStruct((PK_HEADS, N_KEYS, t), BF16)
    ospec = pl.BlockSpec((1, N_KEYS, tm), lambda i, h: (h, 0, i))
    return pl.pallas_call(
        _peer_select_kernel,
        grid=(t // tm, PK_HEADS),
        in_specs=[pl.BlockSpec((tm, d), lambda i, h: (i, 0)),
                  pl.BlockSpec((d, hw), lambda i, h: (0, h)),
                  pl.BlockSpec((1, 2, N_KEYS, PK_HALF), lambda i, h: (h, 0, 0, 0))],
        out_specs=[ospec, ospec, ospec, ospec],
        out_shape=[sel, sel, selb, selb],
        scratch_shapes=[pltpu.VMEM((2, N_KEYS, tm), F32),
                        pltpu.VMEM((N_KEYS, tm), F32),
                        pltpu.VMEM((2, N_KEYS, tm), I32),
                        pltpu.VMEM((2, PK_TOPK, tm), F32),
                        pltpu.VMEM((_CAND_ROWS, tm), F32),
                        pltpu.VMEM((_CAND_ROWS, tm), I32)],
        compiler_params=_params(("parallel", "arbitrary")),
        name="peer_select",
    )(h2, wq, keys)


def _peer_dense_kernel(h_ref, c1_ref, f1_ref, r2_ref, f2_ref, u_ref, v_ref, y_ref, g_ref):
    j = pl.program_id(1)
    te = u_ref.shape[0]
    tm = h_ref.shape[0]

    @pl.when(j == 0)
    def _():
        y_ref[...] = jnp.zeros(y_ref.shape, y_ref.dtype)

    act = lax.dot_general(u_ref[...], h_ref[...], NT_DIMS, preferred_element_type=F32)
    for a in range(te // N_KEYS):
        i1 = j * (te // N_KEYS) + a
        w = jnp.zeros((N_KEYS, tm), BF16)
        for h in range(PK_HEADS):
            c1 = c1_ref[h, pl.ds(i1, 1), :].astype(BF16)
            f1 = f1_ref[h, pl.ds(i1, 1), :].astype(BF16)
            w = w + jnp.where(r2_ref[h] < c1, f2_ref[h] * f1, jnp.zeros((), BF16))
        rows = slice(a * N_KEYS, (a + 1) * N_KEYS)
        g_ref[rows, :] = jax.nn.gelu(act[rows, :]).astype(BF16) * w

    y_ref[...] += lax.dot_general(g_ref[...], v_ref[...], TN_DIMS, preferred_element_type=F32)


def _peer_dense(h2, sel, pu, pv, tm, te):
    t, d = h2.shape
    n_tiles = pu.shape[0] // te
    sspec = pl.BlockSpec((PK_HEADS, N_KEYS, tm), lambda i, j: (0, 0, i))
    return pl.pallas_call(
        _peer_dense_kernel,
        grid=(t // tm, n_tiles),
        in_specs=[pl.BlockSpec((tm, d), lambda i, j: (i, 0)),
                  sspec, sspec, sspec, sspec,
                  pl.BlockSpec((te, d), lambda i, j: (j, 0)),
                  pl.BlockSpec((te, d), lambda i, j: (j, 0))],
        out_specs=pl.BlockSpec((tm, d), lambda i, j: (i, 0)),
        out_shape=jax.ShapeDtypeStruct((t, d), F32),
        scratch_shapes=[pltpu.VMEM((te, tm), BF16)],
        compiler_params=_params(("parallel", "arbitrary")),
        name="peer_dense",
    )(h2, *sel, pu, pv)


def _residual_kernel(x_ref, y_ref, gt_ref, g_ref, o_ref, *, final):
    x2 = x_ref[...] + gt_ref[...] * y_ref[...].reshape(x_ref.shape)
    if final:
        x2 = x2 * lax.rsqrt(jnp.mean(x2 * x2, axis=-1, keepdims=True) + EPS) * g_ref[...]
    o_ref[...] = x2


def _residual(x1, y, y_row0, gt, g_final, nb, r, final):
    b, s, d = x1.shape
    nt = s // r
    y_blk0 = y_row0 // (nb * r)
    return pl.pallas_call(
        functools.partial(_residual_kernel, final=final),
        grid=((b // nb) * nt,),
        in_specs=[pl.BlockSpec((nb, r, d), lambda i: (i // nt, i % nt, 0)),
                  pl.BlockSpec((nb * r, d), lambda i: (y_blk0 + i, 0)),
                  pl.BlockSpec((nb, 1, d), lambda i: (i // nt, 0, 0)),
                  pl.BlockSpec((1, 1, d), lambda i: (0, 0, 0))],
        out_specs=pl.BlockSpec((nb, r, d), lambda i: (i // nt, i % nt, 0)),
        out_shape=jax.ShapeDtypeStruct((b, s, d), F32),
        compiler_params=_params(("parallel",)),
        name="residual",
    )(x1, y, gt, g_final.reshape(1, 1, d))


def _mixers(x, mods, cache, wts, nb, r, gm_rows, gm_tr):
    (g1, w_in, sinks, ln_g, ln_b, ws, gm_b, wa, wb, wo, g2) = wts
    sh1, sc1, gt1, sh2, sc2, _ = mods
    b, s, d = x.shape
    z = _inproj(x, sc1, sh1, g1, w_in, nb, r)
    k = z[:, :, OFF_K:OFF_K + KV_DIM].reshape(b, s, N_KV, HEAD_DIM)
    v = z[:, :, OFF_V:OFF_V + KV_DIM].reshape(b, s, N_KV, HEAD_DIM)
    if cache is None:
        o_a = _attn_prompt(z, sinks)
        o_b, _ = _gmlp(z, ln_g, ln_b, ws, gm_b, gm_rows, gm_tr, emit_vg=False)
        state = (k[:, -WINDOW:], v[:, -WINDOW:])
    else:
        ck, cv = cache
        o_a = _attn_sample(z, ck.reshape(b, ck.shape[1], KV_DIM), cv.reshape(b, cv.shape[1], KV_DIM), sinks)
        o_b, vg = _gmlp(z, ln_g, ln_b, ws[:, :gm_rows, :gm_rows], gm_b[:, :gm_rows], gm_rows, gm_tr,
                        emit_vg=True)
        state = (k, v, vg)
    x1, h2 = _outproj(o_a, o_b, z, x, gt1, sc2, sh2, g2, wa, wb, wo, nb, r)
    return x1, h2.reshape(b * s, d), state


def kernel(x_prompt, x_sample, cache_k, cache_v, c_prompt, c_sample, w_mod, b_mod, g_norm1, w_in, attn_sinks, gm_ln_g, gm_ln_b, gm_ws, gm_b, w_branch_a, w_branch_b, w_out, g_norm2, pk_wq, pk_keys, peer_u, peer_v, g_final):
    depth = w_mod.shape[0]
    bp, sp, d = x_prompt.shape
    bs, ts, _ = x_sample.shape
    assert bp == 1 and sp % 512 == 0 and (bs * ts) % 512 == 0 and ts <= CHUNK and ts % 8 == 0
    xp, xs = x_prompt, x_sample
    nc = bp + bs
    npad = -(-nc // 8) * 8
    kp, vp, ksm, vsm, gsm = [], [], [], [], []
    for l in range(depth):
        c_all = jnp.pad(jnp.concatenate([c_prompt, c_sample], axis=0), ((0, npad - nc), (0, 0)))
        mod = _modulation(c_all, w_mod[l], b_mod[l])
        mods_p = tuple(m.reshape(bp, 1, d) for m in jnp.split(mod[:bp], 6, axis=-1))
        mods_s = tuple(m.reshape(bs, 1, d) for m in jnp.split(mod[bp:nc], 6, axis=-1))
        wts = (g_norm1[l], w_in[l].astype(BF16), attn_sinks[l], gm_ln_g[l], gm_ln_b[l], gm_ws[l], gm_b[l],
               w_branch_a[l].astype(BF16), w_branch_b[l].astype(BF16), w_out[l].astype(BF16), g_norm2[l])
        final = l == depth - 1
        nbs = 512 // ts
        xp1, h2p, (k_p, v_p) = _mixers(xp, mods_p, None, wts, nb=1, r=512, gm_rows=GM_BLOCK, gm_tr=512)
        xs1, h2s, (k_s, v_s, g_s) = _mixers(xs, mods_s, (cache_k[l], cache_v[l]), wts, nb=nbs, r=ts,
                                            gm_rows=ts, gm_tr=ts)
        h2 = jnp.concatenate([h2p, h2s], axis=0)
        sel = _peer_select(h2, pk_wq[l].astype(BF16), pk_keys[l].astype(BF16), tm=512)
        y = _peer_dense(h2, sel, peer_u[l].astype(BF16), peer_v[l].astype(BF16), tm=512, te=1024)
        xp = _residual(xp1, y, 0, mods_p[5], g_final, 1, 512, final)
        xs = _residual(xs1, y, bp * sp, mods_s[5], g_final, nbs, ts, final)
        kp.append(k_p)
        vp.append(v_p)
        ksm.append(k_s)
        vsm.append(v_s)
        gsm.append(g_s)
    return (xp, xs, jnp.stack(kp), jnp.stack(vp), jnp.stack(ksm), jnp.stack(vsm), jnp.stack(gsm))
```

```python
import functools

import jax
import jax.numpy as jnp
from jax import lax
from jax.experimental import pallas as pl
from jax.experimental.pallas import tpu as pltpu

F32 = jnp.float32
BF16 = jnp.bfloat16
I32 = jnp.int32

CHUNK = 64
N_HEADS = 16
N_KV = 4
GQA = N_HEADS // N_KV
HEAD_DIM = 64
ATT_DIM = N_HEADS * HEAD_DIM
KV_DIM = N_KV * HEAD_DIM
WINDOW = 128
WIN_CHUNKS = WINDOW // CHUNK
GM_BLOCK = 128
GM_DIM = 1024
GM_GROUPS = 8
GM_GDIM = GM_DIM // GM_GROUPS
N_KEYS = 128
PK_HEADS = 8
PK_HALF = 128
PK_TOPK = 16
EPS = 1e-6
NEG = -1e30

OFF_Q = 0
OFF_K = ATT_DIM
OFF_V = ATT_DIM + KV_DIM
OFF_GU = ATT_DIM + 2 * KV_DIM
OFF_GV = OFF_GU + GM_DIM
OFF_GA = OFF_GV + GM_DIM

V7X_VMEM_LIMIT_BYTES = 56 * 1024 * 1024

NT_DIMS = (((1,), (1,)), ((), ()))
TN_DIMS = (((0,), (0,)), ((), ()))

_CAND_GROUPS = ((0, 16, 16),) + tuple((k1, 8, PK_TOPK // (k1 + 1)) for k1 in range(1, 8))
_CAND_ROWS = sum(g[1] for g in _CAND_GROUPS) + 8


def _params(sem):
    return pltpu.CompilerParams(dimension_semantics=sem, vmem_limit_bytes=V7X_VMEM_LIMIT_BYTES)


def _mod_kernel(c_ref, w_ref, b_ref, o_ref):
    a = jax.nn.silu(c_ref[...])
    o_ref[...] = jnp.dot(a, w_ref[...], preferred_element_type=F32,
                         precision=lax.Precision.HIGHEST) + b_ref[...]


def _modulation(c, w_mod, b_mod, tn=1024):
    n, d = c.shape
    d6 = w_mod.shape[1]
    return pl.pallas_call(
        _mod_kernel,
        grid=(d6 // tn,),
        in_specs=[pl.BlockSpec((n, d), lambda j: (0, 0)),
                  pl.BlockSpec((d, tn), lambda j: (0, j)),
                  pl.BlockSpec((1, tn), lambda j: (0, j))],
        out_specs=pl.BlockSpec((n, tn), lambda j: (0, j)),
        out_shape=jax.ShapeDtypeStruct((n, d6), F32),
        compiler_params=_params(("parallel",)),
        name="modulation",
    )(c, w_mod, b_mod.reshape(1, d6))


def _rms_mod(x, g, sc, sh):
    y = x * lax.rsqrt(jnp.mean(x * x, axis=-1, keepdims=True) + EPS) * g
    return y * (1.0 + sc) + sh


def _inproj_kernel(x_ref, sc_ref, sh_ref, g_ref, w_ref, o_ref, kv_ref, h_ref):
    nb, r, d = x_ref.shape

    @pl.when(pl.program_id(1) == 0)
    def _():
        h = _rms_mod(x_ref[...], g_ref[...], sc_ref[...], sh_ref[...])
        h_ref[...] = h.reshape(nb * r, d).astype(BF16)

    o = jnp.dot(h_ref[...], w_ref[...], preferred_element_type=F32)
    o_ref[...] = o.reshape(nb, r, o.shape[1]).astype(o_ref.dtype)

    @pl.when(pl.program_id(1) == 0)
    def _():
        kv_ref[...] = o[:, OFF_K:OFF_K + 2 * KV_DIM].reshape(kv_ref.shape)


def _inproj(x, sc, sh, g, w, nb, r, zdtype, tn=OFF_GU):
    b, s, d = x.shape
    n = w.shape[1]
    nt = s // r
    return pl.pallas_call(
        _inproj_kernel,
        grid=((b // nb) * nt, n // tn),
        in_specs=[pl.BlockSpec((nb, r, d), lambda i, j: (i // nt, i % nt, 0)),
                  pl.BlockSpec((nb, 1, d), lambda i, j: (i // nt, 0, 0)),
                  pl.BlockSpec((nb, 1, d), lambda i, j: (i // nt, 0, 0)),
                  pl.BlockSpec((1, 1, d), lambda i, j: (0, 0, 0)),
                  pl.BlockSpec((d, tn), lambda i, j: (0, j))],
        out_specs=[pl.BlockSpec((nb, r, tn), lambda i, j: (i // nt, i % nt, j)),
                   pl.BlockSpec((nb, r, 2 * KV_DIM), lambda i, j: (i // nt, i % nt, 0))],
        out_shape=[jax.ShapeDtypeStruct((b, s, n), zdtype),
                   jax.ShapeDtypeStruct((b, s, 2 * KV_DIM), F32)],
        scratch_shapes=[pltpu.VMEM((nb * r, d), BF16)],
        compiler_params=_params(("parallel", "arbitrary")),
        name="inproj",
    )(x, sc, sh, g.reshape(1, 1, d), w)


def _attn_kernel(q_ref, kp_ref, kc_ref, vp_ref, vc_ref, sink_ref, o_ref, *, band):
    tq = q_ref.shape[1]
    tp = kp_ref.shape[1]
    tk = tp + tq
    q = (q_ref[0] * jnp.asarray(HEAD_DIM ** -0.5, q_ref.dtype)).astype(BF16)
    k = jnp.concatenate([kp_ref[0], kc_ref[0]], axis=0).astype(BF16)
    v = jnp.concatenate([vp_ref[0], vc_ref[0]], axis=0).astype(BF16)
    if band:
        qc = lax.broadcasted_iota(I32, (tq, tk), 0) // CHUNK + tp // CHUNK
        kc = lax.broadcasted_iota(I32, (tq, tk), 1) // CHUNK
        mask = (kc <= qc) & (kc >= qc - WIN_CHUNKS)
        mask = mask & ((kc >= tp // CHUNK) | (pl.program_id(0) > 0))
    for g in range(N_KV):
        kg = k[:, g * HEAD_DIM:(g + 1) * HEAD_DIM]
        vg = v[:, g * HEAD_DIM:(g + 1) * HEAD_DIM]
        for hh in range(GQA):
            h = g * GQA + hh
            qh = q[:, h * HEAD_DIM:(h + 1) * HEAD_DIM]
            s = lax.dot_general(qh, kg, NT_DIMS, preferred_element_type=F32)
            if band:
                s = jnp.where(mask, s, NEG)
            sk = sink_ref[0, h]
            m = jnp.maximum(jnp.max(s, axis=-1, keepdims=True), sk)
            p = jnp.exp(s - m)
            den = jnp.sum(p, axis=-1, keepdims=True) + jnp.exp(sk - m)
            o = jnp.dot(p.astype(BF16), vg, preferred_element_type=F32) / den
            o_ref[0, :, h * HEAD_DIM:(h + 1) * HEAD_DIM] = o.astype(o_ref.dtype)


def _attn_prompt(z, sinks, tq=256):
    b, s, _ = z.shape
    tp = WINDOW
    kblk, vblk = OFF_K // KV_DIM, OFF_V // KV_DIM
    ratio = tq // tp

    def prev(i):
        return jnp.maximum(i * ratio - 1, 0)

    return pl.pallas_call(
        functools.partial(_attn_kernel, band=True),
        grid=(s // tq,),
        in_specs=[pl.BlockSpec((1, tq, ATT_DIM), lambda i: (0, i, 0)),
                  pl.BlockSpec((1, tp, KV_DIM), lambda i: (0, prev(i), kblk)),
                  pl.BlockSpec((1, tq, KV_DIM), lambda i: (0, i, kblk)),
                  pl.BlockSpec((1, tp, KV_DIM), lambda i: (0, prev(i), vblk)),
                  pl.BlockSpec((1, tq, KV_DIM), lambda i: (0, i, vblk)),
                  pl.BlockSpec(memory_space=pltpu.SMEM)],
        out_specs=pl.BlockSpec((1, tq, ATT_DIM), lambda i: (0, i, 0)),
        out_shape=jax.ShapeDtypeStruct((b, s, ATT_DIM), BF16),
        compiler_params=_params(("arbitrary",)),
        name="attn_prompt",
    )(z, z, z, z, z, sinks.reshape(1, N_HEADS))


def _attn_sample(z, ck, cv, sinks):
    b, t, _ = z.shape
    rows = ck.shape[1]
    kblk, vblk = OFF_K // KV_DIM, OFF_V // KV_DIM
    return pl.pallas_call(
        functools.partial(_attn_kernel, band=False),
        grid=(b,),
        in_specs=[pl.BlockSpec((1, t, ATT_DIM), lambda i: (i, 0, 0)),
                  pl.BlockSpec((1, rows, KV_DIM), lambda i: (i, 0, 0)),
                  pl.BlockSpec((1, t, KV_DIM), lambda i: (i, 0, kblk)),
                  pl.BlockSpec((1, rows, KV_DIM), lambda i: (i, 0, 0)),
                  pl.BlockSpec((1, t, KV_DIM), lambda i: (i, 0, vblk)),
                  pl.BlockSpec(memory_space=pltpu.SMEM)],
        out_specs=pl.BlockSpec((1, t, ATT_DIM), lambda i: (i, 0, 0)),
        out_shape=jax.ShapeDtypeStruct((b, t, ATT_DIM), BF16),
        compiler_params=_params(("arbitrary",)),
        name="attn_sample",
    )(z, ck, z, cv, z, sinks.reshape(1, N_HEADS))


def _gmlp_kernel(gu0_ref, gu1_ref, gv0_ref, gv1_ref, lg_ref, lb_ref, w_ref, b_ref, o_ref, *vg_out, rows):
    nblk = gu0_ref.shape[1] // rows
    gv = jax.nn.gelu(jnp.concatenate([gv0_ref[0], gv1_ref[0]], axis=-1).astype(F32))
    mu = jnp.mean(gv, axis=-1, keepdims=True)
    var = jnp.mean(jnp.square(gv - mu), axis=-1, keepdims=True)
    vg = (gv - mu) * lax.rsqrt(var + EPS) * lg_ref[...] + lb_ref[...]
    if vg_out:
        vg_out[0][0] = vg
    u = jax.nn.gelu(jnp.concatenate([gu0_ref[0], gu1_ref[0]], axis=-1).astype(F32))
    vgb = vg.astype(BF16)
    ii = lax.broadcasted_iota(I32, (rows, rows), 0) // CHUNK
    jj = lax.broadcasted_iota(I32, (rows, rows), 1) // CHUNK
    wmask = jj <= ii
    for g in range(GM_GROUPS):
        wg = jnp.where(wmask, w_ref[g], 0.0).astype(BF16)
        cols = slice(g * GM_GDIM, (g + 1) * GM_GDIM)
        for bi in range(nblk):
            rs = slice(bi * rows, (bi + 1) * rows)
            mixed = jnp.dot(wg, vgb[rs, cols], preferred_element_type=F32) + b_ref[g]
            o_ref[0, rs, cols] = (u[rs, cols] * mixed).astype(o_ref.dtype)


def _gmlp(z, ln_g, ln_b, ws, bias, rows, tr, emit_vg):
    b, s, _ = z.shape
    half = GM_DIM // 2
    ublk, vblk = OFF_GU // half, OFF_GV // half
    nt = s // tr

    def cols(blk):
        return pl.BlockSpec((1, tr, half), lambda i: (i // nt, i % nt, blk))

    out_shape = [jax.ShapeDtypeStruct((b, s, GM_DIM), BF16)]
    out_specs = [pl.BlockSpec((1, tr, GM_DIM), lambda i: (i // nt, i % nt, 0))]
    if emit_vg:
        out_shape.append(jax.ShapeDtypeStruct((b, s, GM_DIM), F32))
        out_specs.append(pl.BlockSpec((1, tr, GM_DIM), lambda i: (i // nt, i % nt, 0)))
    res = pl.pallas_call(
        functools.partial(_gmlp_kernel, rows=rows),
        grid=(b * nt,),
        in_specs=[cols(ublk), cols(ublk + 1), cols(vblk), cols(vblk + 1),
                  pl.BlockSpec((1, GM_DIM), lambda i: (0, 0)),
                  pl.BlockSpec((1, GM_DIM), lambda i: (0, 0)),
                  pl.BlockSpec((GM_GROUPS, rows, rows), lambda i: (0, 0, 0)),
                  pl.BlockSpec((GM_GROUPS, rows, 1), lambda i: (0, 0, 0))],
        out_specs=out_specs,
        out_shape=out_shape,
        compiler_params=_params(("parallel",)),
        name="gmlp",
    )(z, z, z, z, ln_g.reshape(1, GM_DIM), ln_b.reshape(1, GM_DIM), ws, bias.reshape(GM_GROUPS, rows, 1))
    return res if emit_vg else (res[0], None)


GATE_BLK = 512


def _outproj_kernel(oa_ref, ob_ref, ga0_ref, ga1_ref, gb0_ref, gb1_ref, x_ref, gt_ref, sc_ref, sh_ref, g2_ref,
                    wa_ref, wb_ref, wo_ref, x1_ref, h2_ref):
    nb, r, d = x_ref.shape
    j = pl.program_id(1)
    oa = oa_ref[...].reshape(nb * r, oa_ref.shape[2])
    ob = ob_ref[...].reshape(nb * r, ob_ref.shape[2])
    t1 = jnp.dot(oa, wa_ref[...], preferred_element_type=F32)
    t2 = jnp.dot(ob, wb_ref[...], preferred_element_type=F32)
    ga = jnp.concatenate([ga0_ref[...], ga1_ref[...]], axis=-1).reshape(nb * r, 2 * GATE_BLK).astype(F32)
    gb = jnp.concatenate([gb0_ref[...], gb1_ref[...]], axis=-1).reshape(nb * r, 2 * GATE_BLK).astype(F32)
    mix = jax.nn.sigmoid(ga) * t1 + jax.nn.sigmoid(gb) * t2
    part = jnp.dot(mix.astype(BF16), wo_ref[...], preferred_element_type=F32).reshape(nb, r, d)

    @pl.when(j == 0)
    def _():
        x1_ref[...] = part

    @pl.when(j > 0)
    def _():
        x1_ref[...] += part

    @pl.when(j == pl.num_programs(1) - 1)
    def _():
        x1 = x_ref[...] + gt_ref[...] * x1_ref[...]
        x1_ref[...] = x1
        h2_ref[...] = _rms_mod(x1, g2_ref[...], sc_ref[...], sh_ref[...]).astype(h2_ref.dtype)


def _outproj(oa, ob, z, x, gt, sc, sh, g2, wa, wb, wo, nb, r):
    b, s, d = x.shape
    nt = s // r
    tc = 2 * GATE_BLK
    ga0, gb0 = OFF_GA // GATE_BLK, (OFF_GA + d) // GATE_BLK

    def rows(i, j):
        return (i // nt, i % nt, 0)

    def per_stream(i, j):
        return (i // nt, 0, 0)

    def gate(blk0, k):
        return pl.BlockSpec((nb, r, GATE_BLK), lambda i, j: (i // nt, i % nt, blk0 + 2 * j + k))

    return pl.pallas_call(
        _outproj_kernel,
        grid=((b // nb) * nt, d // tc),
        in_specs=[pl.BlockSpec((nb, r, ATT_DIM), rows),
                  pl.BlockSpec((nb, r, GM_DIM), rows),
                  gate(ga0, 0), gate(ga0, 1), gate(gb0, 0), gate(gb0, 1),
                  pl.BlockSpec((nb, r, d), rows),
                  pl.BlockSpec((nb, 1, d), per_stream),
                  pl.BlockSpec((nb, 1, d), per_stream),
                  pl.BlockSpec((nb, 1, d), per_stream),
                  pl.BlockSpec((1, 1, d), lambda i, j: (0, 0, 0)),
                  pl.BlockSpec((ATT_DIM, tc), lambda i, j: (0, j)),
                  pl.BlockSpec((GM_DIM, tc), lambda i, j: (0, j)),
                  pl.BlockSpec((tc, d), lambda i, j: (j, 0))],
        out_specs=[pl.BlockSpec((nb, r, d), rows), pl.BlockSpec((nb, r, d), rows)],
        out_shape=[jax.ShapeDtypeStruct((b, s, d), F32), jax.ShapeDtypeStruct((b, s, d), BF16)],
        compiler_params=_params(("parallel", "arbitrary")),
        name="outproj",
    )(oa, ob, z, z, z, z, x, gt, sc, sh, g2.reshape(1, 1, d), wa, wb, wo)


def _peer_select_kernel(h_ref, wq_ref, keys_ref, c1_ref, f1_ref, r2_ref, f2_ref,
                        s_ref, work_ref, rank_ref, sv_ref, cand_ref, fid_ref, cv_ref, cnt_ref, z_ref):
    tm = h_ref.shape[0]
    neg_inf = jnp.float32(-jnp.inf)
    q = jnp.dot(h_ref[...], wq_ref[...], preferred_element_type=F32).astype(BF16)
    row = lax.broadcasted_iota(I32, (N_KEYS, tm), 0)

    def n_tokens_not_topk(selected):
        n_sel = jnp.sum(selected.astype(I32), axis=0, keepdims=True)
        return jnp.sum((n_sel != PK_TOPK).astype(I32))

    for p in range(2):
        s = lax.dot_general(keys_ref[0, p], q[:, p * PK_HALF:(p + 1) * PK_HALF], NT_DIMS,
                            preferred_element_type=F32)
        s_ref[p] = s

        def next_max(k, m_prev, p=p):
            w = s_ref[p]
            m = jnp.max(jnp.where(w < m_prev, w, neg_inf), axis=0, keepdims=True)
            sv_ref[p, pl.ds(k, 1), :] = m
            return m

        m_last = lax.fori_loop(0, PK_TOPK, next_max, jnp.full((1, tm), jnp.inf, F32))
        rank = jnp.zeros((N_KEYS, tm), I32)
        for k in range(PK_TOPK):
            rank = rank + (sv_ref[p, k:k + 1, :] > s).astype(I32)
        rank_ref[p] = rank

        @pl.when(n_tokens_not_topk(s >= m_last) > 0)
        def _(p=p):
            def extract_one(k, carry):
                w = work_ref[...]
                m = jnp.max(w, axis=0, keepdims=True)
                idx = jnp.min(jnp.where(w == m, row, N_KEYS), axis=0, keepdims=True)
                hit = row == idx
                work_ref[...] = jnp.where(hit, neg_inf, w)
                rank_ref[p] = jnp.where(hit, k, rank_ref[p])
                sv_ref[p, pl.ds(k, 1), :] = m
                return carry

            work_ref[...] = s_ref[p]
            rank_ref[p] = jnp.full((N_KEYS, tm), PK_TOPK, I32)
            lax.fori_loop(0, PK_TOPK, extract_one, 0)

    sv1 = sv_ref[0]
    sv2 = sv_ref[1]
    r8 = lax.broadcasted_iota(I32, (8, tm), 0)
    r16 = lax.broadcasted_iota(I32, (16, tm), 0)
    off = 0
    for k1, nrows, nused in _CAND_GROUPS:
        rr = r16 if nrows == 16 else r8
        val = sv1[k1:k1 + 1, :] + sv2[0:nrows, :]
        cand_ref[off:off + nrows, :] = jnp.where(rr < nused, val, neg_inf)
        off += nrows
    cand_ref[off:off + 8, :] = sv1[8:16, :] + sv2[0:1, :]
    m0 = sv1[0:1, :] + sv2[0:1, :]

    def next_max2(k, m_prev):
        w = cand_ref[...]
        m = jnp.max(jnp.where(w < m_prev, w, neg_inf), axis=0, keepdims=True)
        cv_ref[pl.ds(k, 1), :] = m
        return m

    tau = lax.fori_loop(0, PK_TOPK, next_max2, jnp.full((1, tm), jnp.inf, F32))
    picked = cand_ref[...] >= tau
    counts = []
    off = 0
    for k1, nrows, nused in _CAND_GROUPS:
        counts.append(jnp.sum(picked[off:off + nrows].astype(I32), axis=0, keepdims=True))
        off += nrows
    counts.append(picked[off:off + 8].astype(I32))
    cnt_ref[...] = jnp.concatenate(counts, axis=0)
    z_ref[...] = jnp.sum(jnp.exp(cv_ref[...] - m0), axis=0, keepdims=True)

    @pl.when(n_tokens_not_topk(picked) > 0)
    def _():
        off = 0
        for k1, nrows, nused in _CAND_GROUPS:
            rr = r16 if nrows == 16 else r8
            fid_ref[off:off + nrows, :] = k1 * PK_TOPK + rr
            off += nrows
        fid_ref[off:off + 8, :] = (r8 + 8) * PK_TOPK
        big = jnp.int32(PK_TOPK * PK_TOPK)

        def extract2(k, carry):
            cnt, zsum = carry
            w = cand_ref[...]
            fid = fid_ref[...]
            m = jnp.max(w, axis=0, keepdims=True)
            idx = jnp.min(jnp.where(w == m, fid, big), axis=0, keepdims=True)
            cand_ref[...] = jnp.where(fid == idx, neg_inf, w)
            cnt = cnt + (r16 == lax.shift_right_logical(idx, 4)).astype(I32)
            zsum = zsum + jnp.exp(m - m0)
            return cnt, zsum

        cnt, zsum = lax.fori_loop(0, PK_TOPK, extract2,
                                  (jnp.zeros((PK_TOPK, tm), I32), jnp.zeros((1, tm), F32)))
        cnt_ref[...] = cnt
        z_ref[...] = zsum

    cnt = cnt_ref[...]
    rank1 = rank_ref[0]
    c1 = jnp.zeros((N_KEYS, tm), I32)
    for k1 in range(PK_TOPK):
        c1 = jnp.where(rank1 == k1, cnt[k1:k1 + 1, :], c1)
    c1_ref[0] = c1.astype(c1_ref.dtype)
    f1_ref[0] = jnp.exp(s_ref[0] - sv_ref[0, 0:1, :])
    r2_ref[0] = rank_ref[1].astype(r2_ref.dtype)
    f2_ref[0] = (jnp.exp(s_ref[1] - sv_ref[1, 0:1, :]) / z_ref[...]).astype(f2_ref.dtype)


def _peer_select(h2, wq, keys, tm):
    t, d = h2.shape
    hw = 2 * PK_HALF
    sel = jax.ShapeDtypeStruct((PK_HEADS, N_KEYS, t), F32)
    selb = jax.ShapeDtypeStruct((PK_HEADS, N_KEYS, t), BF16)
    ospec = pl.BlockSpec((1, N_KEYS, tm), lambda i, h: (h, 0, i))
    return pl.pallas_call(
        _peer_select_kernel,
        grid=(t // tm, PK_HEADS),
        in_specs=[pl.BlockSpec((tm, d), lambda i, h: (i, 0)),
                  pl.BlockSpec((d, hw), lambda i, h: (0, h)),
                  pl.BlockSpec((1, 2, N_KEYS, PK_HALF), lambda i, h: (h, 0, 0, 0))],
        out_specs=[ospec, ospec, ospec, ospec],
        out_shape=[sel, sel, selb, selb],
        scratch_shapes=[pltpu.VMEM((2, N_KEYS, tm), F32),
                        pltpu.VMEM((N_KEYS, tm), F32),
                        pltpu.VMEM((2, N_KEYS, tm), I32),
                        pltpu.VMEM((2, PK_TOPK, tm), F32),
                        pltpu.VMEM((_CAND_ROWS, tm), F32),
                        pltpu.VMEM((_CAND_ROWS, tm), I32),
                        pltpu.VMEM((PK_TOPK, tm), F32),
                        pltpu.VMEM((PK_TOPK, tm), I32),
                        pltpu.VMEM((1, tm), F32)],
        compiler_params=_params(("parallel", "arbitrary")),
        name="peer_select",
    )(h2, wq, keys)


def _peer_dense_kernel(h_ref, c1_ref, f1_ref, r2_ref, f2_ref, u_ref, v_ref, y_ref, g_ref):
    j = pl.program_id(1)
    te = u_ref.shape[0]
    tm = h_ref.shape[0]

    @pl.when(j == 0)
    def _():
        y_ref[...] = jnp.zeros(y_ref.shape, y_ref.dtype)

    act = lax.dot_general(u_ref[...], h_ref[...], NT_DIMS, preferred_element_type=F32)
    for a in range(te // N_KEYS):
        i1 = j * (te // N_KEYS) + a
        w = jnp.zeros((N_KEYS, tm), BF16)
        for h in range(PK_HEADS):
            c1 = c1_ref[h, pl.ds(i1, 1), :].astype(BF16)
            f1 = f1_ref[h, pl.ds(i1, 1), :].astype(BF16)
            w = w + jnp.where(r2_ref[h] < c1, f2_ref[h] * f1, jnp.zeros((), BF16))
        rows = slice(a * N_KEYS, (a + 1) * N_KEYS)
        g_ref[rows, :] = jax.nn.gelu(act[rows, :].astype(BF16)) * w

    y_ref[...] += lax.dot_general(g_ref[...], v_ref[...], TN_DIMS, preferred_element_type=F32)


def _peer_dense(h2, sel, pu, pv, tm, te):
    t, d = h2.shape
    sspec = pl.BlockSpec((PK_HEADS, N_KEYS, tm), lambda i, j: (0, 0, i))
    return pl.pallas_call(
        _peer_dense_kernel,
        grid=(t // tm, pu.shape[0] // te),
        in_specs=[pl.BlockSpec((tm, d), lambda i, j: (i, 0)),
                  sspec, sspec, sspec, sspec,
                  pl.BlockSpec((te, d), lambda i, j: (j, 0)),
                  pl.BlockSpec((te, d), lambda i, j: (j, 0))],
        out_specs=pl.BlockSpec((tm, d), lambda i, j: (i, 0)),
        out_shape=jax.ShapeDtypeStruct((t, d), F32),
        scratch_shapes=[pltpu.VMEM((te, tm), BF16)],
        compiler_params=_params(("parallel", "arbitrary")),
        name="peer_dense",
    )(h2, *sel, pu, pv)


def _residual_kernel(x_ref, y_ref, gt_ref, g_ref, o_ref, *, final):
    x2 = x_ref[...] + gt_ref[...] * y_ref[...].reshape(x_ref.shape)
    if final:
        x2 = x2 * lax.rsqrt(jnp.mean(x2 * x2, axis=-1, keepdims=True) + EPS) * g_ref[...]
    o_ref[...] = x2


def _residual(x1, y, y_row0, gt, g_final, nb, r, final):
    b, s, d = x1.shape
    nt = s // r
    y_blk0 = y_row0 // (nb * r)
    return pl.pallas_call(
        functools.partial(_residual_kernel, final=final),
        grid=((b // nb) * nt,),
        in_specs=[pl.BlockSpec((nb, r, d), lambda i: (i // nt, i % nt, 0)),
                  pl.BlockSpec((nb * r, d), lambda i: (y_blk0 + i, 0)),
                  pl.BlockSpec((nb, 1, d), lambda i: (i // nt, 0, 0)),
                  pl.BlockSpec((1, 1, d), lambda i: (0, 0, 0))],
        out_specs=pl.BlockSpec((nb, r, d), lambda i: (i // nt, i % nt, 0)),
        out_shape=jax.ShapeDtypeStruct((b, s, d), F32),
        compiler_params=_params(("parallel",)),
        name="residual",
    )(x1, y, gt, g_final.reshape(1, 1, d))


def _mixers(x, mods, cache, wts, nb, r, gm_rows, gm_tr):
    (g1, w_in, sinks, ln_g, ln_b, ws, gm_b, wa, wb, wo, g2) = wts
    sh1, sc1, gt1, sh2, sc2, _ = mods
    b, s, d = x.shape
    if cache is None:
        z, kv = _inproj(x, sc1, sh1, g1, w_in, nb, 2 * r, BF16)
        kv = kv[:, -WINDOW:]
    else:
        z, kv = _inproj(x, sc1, sh1, g1, w_in, nb, r, F32)
    k = kv[:, :, :KV_DIM].reshape(b, -1, N_KV, HEAD_DIM)
    v = kv[:, :, KV_DIM:].reshape(b, -1, N_KV, HEAD_DIM)
    if cache is None:
        o_a = _attn_prompt(z, sinks)
        o_b, _ = _gmlp(z, ln_g, ln_b, ws, gm_b, gm_rows, gm_tr, emit_vg=False)
        state = (k, v)
    else:
        ck, cv = cache
        o_a = _attn_sample(z, ck.reshape(b, ck.shape[1], KV_DIM), cv.reshape(b, cv.shape[1], KV_DIM), sinks)
        o_b, vg = _gmlp(z, ln_g, ln_b, ws[:, :gm_rows, :gm_rows], gm_b[:, :gm_rows], gm_rows, gm_tr,
                        emit_vg=True)
        state = (k, v, vg)
    x1, h2 = _outproj(o_a, o_b, z, x, gt1, sc2, sh2, g2, wa, wb, wo, nb, r)
    return x1, h2.reshape(b * s, d), state


def kernel(x_prompt, x_sample, cache_k, cache_v, c_prompt, c_sample, w_mod, b_mod, g_norm1, w_in, attn_sinks, gm_ln_g, gm_ln_b, gm_ws, gm_b, w_branch_a, w_branch_b, w_out, g_norm2, pk_wq, pk_keys, peer_u, peer_v, g_final):
    depth = w_mod.shape[0]
    bp, sp, d = x_prompt.shape
    bs, ts, _ = x_sample.shape
    assert bp == 1 and sp % 1024 == 0 and (bs * ts) % 512 == 0 and ts <= CHUNK and ts % 8 == 0
    xp, xs = x_prompt, x_sample
    nc = bp + bs
    npad = -(-nc // 8) * 8
    kp, vp, ksm, vsm, gsm = [], [], [], [], []
    for l in range(depth):
        c_all = jnp.pad(jnp.concatenate([c_prompt, c_sample], axis=0), ((0, npad - nc), (0, 0)))
        mod = _modulation(c_all, w_mod[l], b_mod[l])
        mods_p = tuple(m.reshape(bp, 1, d) for m in jnp.split(mod[:bp], 6, axis=-1))
        mods_s = tuple(m.reshape(bs, 1, d) for m in jnp.split(mod[bp:nc], 6, axis=-1))
        wts = (g_norm1[l], w_in[l].astype(BF16), attn_sinks[l], gm_ln_g[l], gm_ln_b[l], gm_ws[l], gm_b[l],
               w_branch_a[l].astype(BF16), w_branch_b[l].astype(BF16), w_out[l].astype(BF16), g_norm2[l])
        final = l == depth - 1
        nbs = 512 // ts
        xp1, h2p, (k_p, v_p) = _mixers(xp, mods_p, None, wts, nb=1, r=512, gm_rows=GM_BLOCK, gm_tr=512)
        xs1, h2s, (k_s, v_s, g_s) = _mixers(xs, mods_s, (cache_k[l], cache_v[l]), wts, nb=nbs, r=ts,
                                            gm_rows=ts, gm_tr=ts)
        h2 = jnp.concatenate([h2p, h2s], axis=0)
        sel = _peer_select(h2, pk_wq[l].astype(BF16), pk_keys[l].astype(BF16), tm=512)
        y = _peer_dense(h2, sel, peer_u[l].astype(BF16), peer_v[l].astype(BF16), tm=512, te=1024)
        xp = _residual(xp1, y, 0, mods_p[5], g_final, 1, 512, final)
        xs = _residual(xs1, y, bp * sp, mods_s[5], g_final, nbs, ts, final)
        kp.append(k_p)
        vp.append(v_p)
        ksm.append(k_s)
        vsm.append(v_s)
        gsm.append(g_s)
    return (xp, xs, jnp.stack(kp), jnp.stack(vp), jnp.stack(ksm), jnp.stack(vsm), jnp.stack(gsm))
```

```python
import functools

import jax
import jax.numpy as jnp
from jax import lax
from jax.experimental import pallas as pl
from jax.experimental.pallas import tpu as pltpu

F32 = jnp.float32
BF16 = jnp.bfloat16
I32 = jnp.int32

CHUNK = 64
N_HEADS = 16
N_KV = 4
GQA = N_HEADS // N_KV
HEAD_DIM = 64
ATT_DIM = N_HEADS * HEAD_DIM
KV_DIM = N_KV * HEAD_DIM
WINDOW = 128
WIN_CHUNKS = WINDOW // CHUNK
GM_BLOCK = 128
GM_DIM = 1024
GM_GROUPS = 8
GM_GDIM = GM_DIM // GM_GROUPS
N_KEYS = 128
PK_HEADS = 8
PK_HALF = 128
PK_TOPK = 16
EPS = 1e-6
NEG = -1e30

OFF_Q = 0
OFF_K = ATT_DIM
OFF_V = ATT_DIM + KV_DIM
OFF_GU = ATT_DIM + 2 * KV_DIM
OFF_GV = OFF_GU + GM_DIM
OFF_GA = OFF_GV + GM_DIM

V7X_VMEM_LIMIT_BYTES = 56 * 1024 * 1024

NT_DIMS = (((1,), (1,)), ((), ()))
TN_DIMS = (((0,), (0,)), ((), ()))

_CAND_GROUPS = ((0, 16, 16),) + tuple((k1, 8, PK_TOPK // (k1 + 1)) for k1 in range(1, 8))
_CAND_ROWS = sum(g[1] for g in _CAND_GROUPS) + 8


def _params(sem):
    return pltpu.CompilerParams(dimension_semantics=sem, vmem_limit_bytes=V7X_VMEM_LIMIT_BYTES)


def _mod_kernel(c_ref, w_ref, b_ref, o_ref):
    a = jax.nn.silu(c_ref[...])
    o_ref[...] = jnp.dot(a, w_ref[...], preferred_element_type=F32,
                         precision=lax.Precision.HIGHEST) + b_ref[...]


def _modulation(c, w_mod, b_mod, tn=1024):
    n, d = c.shape
    d6 = w_mod.shape[1]
    return pl.pallas_call(
        _mod_kernel,
        grid=(d6 // tn,),
        in_specs=[pl.BlockSpec((n, d), lambda j: (0, 0)),
                  pl.BlockSpec((d, tn), lambda j: (0, j)),
                  pl.BlockSpec((1, tn), lambda j: (0, j))],
        out_specs=pl.BlockSpec((n, tn), lambda j: (0, j)),
        out_shape=jax.ShapeDtypeStruct((n, d6), F32),
        compiler_params=_params(("parallel",)),
        name="modulation",
    )(c, w_mod, b_mod.reshape(1, d6))


def _rms_mod(x, g, sc, sh):
    y = x * lax.rsqrt(jnp.mean(x * x, axis=-1, keepdims=True) + EPS) * g
    return y * (1.0 + sc) + sh


def _inproj_kernel(x_ref, sc_ref, sh_ref, g_ref, w_ref, o_ref, kv_ref, h_ref):
    nb, r, d = x_ref.shape

    @pl.when(pl.program_id(1) == 0)
    def _():
        h = _rms_mod(x_ref[...], g_ref[...], sc_ref[...], sh_ref[...])
        h_ref[...] = h.reshape(nb * r, d).astype(BF16)

    o = jnp.dot(h_ref[...], w_ref[...], preferred_element_type=F32)
    o_ref[...] = o.reshape(nb, r, o.shape[1]).astype(o_ref.dtype)

    @pl.when(pl.program_id(1) == 0)
    def _():
        kv_ref[...] = o[:, OFF_K:OFF_K + 2 * KV_DIM].reshape(kv_ref.shape)


def _inproj(x, sc, sh, g, w, nb, r, zdtype, tn=OFF_GU):
    b, s, d = x.shape
    n = w.shape[1]
    nt = s // r
    return pl.pallas_call(
        _inproj_kernel,
        grid=((b // nb) * nt, n // tn),
        in_specs=[pl.BlockSpec((nb, r, d), lambda i, j: (i // nt, i % nt, 0)),
                  pl.BlockSpec((nb, 1, d), lambda i, j: (i // nt, 0, 0)),
                  pl.BlockSpec((nb, 1, d), lambda i, j: (i // nt, 0, 0)),
                  pl.BlockSpec((1, 1, d), lambda i, j: (0, 0, 0)),
                  pl.BlockSpec((d, tn), lambda i, j: (0, j))],
        out_specs=[pl.BlockSpec((nb, r, tn), lambda i, j: (i // nt, i % nt, j)),
                   pl.BlockSpec((nb, r, 2 * KV_DIM), lambda i, j: (i // nt, i % nt, 0))],
        out_shape=[jax.ShapeDtypeStruct((b, s, n), zdtype),
                   jax.ShapeDtypeStruct((b, s, 2 * KV_DIM), F32)],
        scratch_shapes=[pltpu.VMEM((nb * r, d), BF16)],
        compiler_params=_params(("parallel", "arbitrary")),
        name="inproj",
    )(x, sc, sh, g.reshape(1, 1, d), w)


def _attn_band_kernel(q_ref, kp_ref, kc_ref, vp_ref, vc_ref, sink_ref, o_ref, *, prev_missing_at_start):
    tq = q_ref.shape[1]
    tp = kp_ref.shape[1]
    chunk = min(CHUNK, tq)
    span = tp + chunk
    q = (q_ref[0] * jnp.asarray(HEAD_DIM ** -0.5, q_ref.dtype)).astype(BF16)
    k = jnp.concatenate([kp_ref[0], kc_ref[0]], axis=0).astype(BF16)
    v = jnp.concatenate([vp_ref[0], vc_ref[0]], axis=0).astype(BF16)
    nrow = N_HEADS * chunk
    hrow = lax.broadcasted_iota(I32, (nrow, 1), 0) // chunk
    kcol = lax.broadcasted_iota(I32, (nrow, span), 1)
    first = pl.program_id(0) == 0
    sk = jnp.zeros((nrow, 1), F32)
    for h in range(N_HEADS):
        sk = jnp.where(hrow == h, sink_ref[0, h], sk)
    for c in range(tq // chunk):
        rows = slice(c * chunk, (c + 1) * chunk)
        keys = slice(c * chunk, c * chunk + span)
        scores = []
        for g in range(N_KV):
            qg = jnp.concatenate([q[rows, (g * GQA + hh) * HEAD_DIM:(g * GQA + hh + 1) * HEAD_DIM]
                                  for hh in range(GQA)], axis=0)
            scores.append(lax.dot_general(qg, k[keys, g * HEAD_DIM:(g + 1) * HEAD_DIM], NT_DIMS,
                                          preferred_element_type=F32))
        s = jnp.concatenate(scores, axis=0)
        if prev_missing_at_start and c * chunk < tp:
            s = jnp.where(first & (kcol < tp - c * chunk), NEG, s)
        m = jnp.maximum(jnp.max(s, axis=-1, keepdims=True), sk)
        p = jnp.exp(s - m)
        inv = 1.0 / (jnp.sum(p, axis=-1, keepdims=True) + jnp.exp(sk - m))
        pb = p.astype(BF16)
        for g in range(N_KV):
            grows = slice(g * GQA * chunk, (g + 1) * GQA * chunk)
            o = jnp.dot(pb[grows], v[keys, g * HEAD_DIM:(g + 1) * HEAD_DIM],
                        preferred_element_type=F32) * inv[grows]
            for hh in range(GQA):
                h = g * GQA + hh
                o_ref[0, rows, h * HEAD_DIM:(h + 1) * HEAD_DIM] = o[hh * chunk:(hh + 1) * chunk].astype(o_ref.dtype)


def _attn_prompt(z, sinks, tq=256):
    b, s, _ = z.shape
    tp = WINDOW
    kblk, vblk = OFF_K // KV_DIM, OFF_V // KV_DIM
    ratio = tq // tp

    def prev(i):
        return jnp.maximum(i * ratio - 1, 0)

    return pl.pallas_call(
        functools.partial(_attn_band_kernel, prev_missing_at_start=True),
        grid=(s // tq,),
        in_specs=[pl.BlockSpec((1, tq, ATT_DIM), lambda i: (0, i, 0)),
                  pl.BlockSpec((1, tp, KV_DIM), lambda i: (0, prev(i), kblk)),
                  pl.BlockSpec((1, tq, KV_DIM), lambda i: (0, i, kblk)),
                  pl.BlockSpec((1, tp, KV_DIM), lambda i: (0, prev(i), vblk)),
                  pl.BlockSpec((1, tq, KV_DIM), lambda i: (0, i, vblk)),
                  pl.BlockSpec(memory_space=pltpu.SMEM)],
        out_specs=pl.BlockSpec((1, tq, ATT_DIM), lambda i: (0, i, 0)),
        out_shape=jax.ShapeDtypeStruct((b, s, ATT_DIM), BF16),
        compiler_params=_params(("arbitrary",)),
        name="attn_prompt",
    )(z, z, z, z, z, sinks.reshape(1, N_HEADS))


def _attn_sample(z, ck, cv, sinks):
    b, t, _ = z.shape
    rows = ck.shape[1]
    kblk, vblk = OFF_K // KV_DIM, OFF_V // KV_DIM
    return pl.pallas_call(
        functools.partial(_attn_band_kernel, prev_missing_at_start=False),
        grid=(b,),
        in_specs=[pl.BlockSpec((1, t, ATT_DIM), lambda i: (i, 0, 0)),
                  pl.BlockSpec((1, rows, KV_DIM), lambda i: (i, 0, 0)),
                  pl.BlockSpec((1, t, KV_DIM), lambda i: (i, 0, kblk)),
                  pl.BlockSpec((1, rows, KV_DIM), lambda i: (i, 0, 0)),
                  pl.BlockSpec((1, t, KV_DIM), lambda i: (i, 0, vblk)),
                  pl.BlockSpec(memory_space=pltpu.SMEM)],
        out_specs=pl.BlockSpec((1, t, ATT_DIM), lambda i: (i, 0, 0)),
        out_shape=jax.ShapeDtypeStruct((b, t, ATT_DIM), BF16),
        compiler_params=_params(("arbitrary",)),
        name="attn_sample",
    )(z, ck, z, cv, z, sinks.reshape(1, N_HEADS))


def _gmlp_kernel(gu0_ref, gu1_ref, gv0_ref, gv1_ref, lg_ref, lb_ref, w_ref, b_ref, o_ref, *vg_out, rows):
    nblk = gu0_ref.shape[1] // rows
    gv = jax.nn.gelu(jnp.concatenate([gv0_ref[0], gv1_ref[0]], axis=-1).astype(F32))
    mu = jnp.mean(gv, axis=-1, keepdims=True)
    var = jnp.mean(jnp.square(gv - mu), axis=-1, keepdims=True)
    vg = (gv - mu) * lax.rsqrt(var + EPS) * lg_ref[...] + lb_ref[...]
    if vg_out:
        vg_out[0][0] = vg
    u = jax.nn.gelu(jnp.concatenate([gu0_ref[0], gu1_ref[0]], axis=-1).astype(F32))
    vgb = vg.astype(BF16)
    ii = lax.broadcasted_iota(I32, (rows, rows), 0) // CHUNK
    jj = lax.broadcasted_iota(I32, (rows, rows), 1) // CHUNK
    wmask = jj <= ii
    for g in range(GM_GROUPS):
        wg = jnp.where(wmask, w_ref[g], 0.0).astype(BF16)
        cols = slice(g * GM_GDIM, (g + 1) * GM_GDIM)
        for bi in range(nblk):
            rs = slice(bi * rows, (bi + 1) * rows)
            mixed = jnp.dot(wg, vgb[rs, cols], preferred_element_type=F32) + b_ref[g]
            o_ref[0, rs, cols] = (u[rs, cols] * mixed).astype(o_ref.dtype)


def _gmlp(z, ln_g, ln_b, ws, bias, rows, tr, emit_vg):
    b, s, _ = z.shape
    half = GM_DIM // 2
    ublk, vblk = OFF_GU // half, OFF_GV // half
    nt = s // tr

    def cols(blk):
        return pl.BlockSpec((1, tr, half), lambda i: (i // nt, i % nt, blk))

    out_shape = [jax.ShapeDtypeStruct((b, s, GM_DIM), BF16)]
    out_specs = [pl.BlockSpec((1, tr, GM_DIM), lambda i: (i // nt, i % nt, 0))]
    if emit_vg:
        out_shape.append(jax.ShapeDtypeStruct((b, s, GM_DIM), F32))
        out_specs.append(pl.BlockSpec((1, tr, GM_DIM), lambda i: (i // nt, i % nt, 0)))
    res = pl.pallas_call(
        functools.partial(_gmlp_kernel, rows=rows),
        grid=(b * nt,),
        in_specs=[cols(ublk), cols(ublk + 1), cols(vblk), cols(vblk + 1),
                  pl.BlockSpec((1, GM_DIM), lambda i: (0, 0)),
                  pl.BlockSpec((1, GM_DIM), lambda i: (0, 0)),
                  pl.BlockSpec((GM_GROUPS, rows, rows), lambda i: (0, 0, 0)),
                  pl.BlockSpec((GM_GROUPS, rows, 1), lambda i: (0, 0, 0))],
        out_specs=out_specs,
        out_shape=out_shape,
        compiler_params=_params(("parallel",)),
        name="gmlp",
    )(z, z, z, z, ln_g.reshape(1, GM_DIM), ln_b.reshape(1, GM_DIM), ws, bias.reshape(GM_GROUPS, rows, 1))
    return res if emit_vg else (res[0], None)


GATE_BLK = 512


def _outproj_kernel(oa_ref, ob_ref, ga0_ref, ga1_ref, gb0_ref, gb1_ref, x_ref, gt_ref, sc_ref, sh_ref, g2_ref,
                    wa_ref, wb_ref, wo_ref, *rest):
    x1_ref, h2_ref = rest[-2:]
    nb, r, d = x_ref.shape
    j = pl.program_id(1)
    oa = oa_ref[...].reshape(nb * r, oa_ref.shape[2])
    ob = ob_ref[...].reshape(nb * r, ob_ref.shape[2])
    t1 = jnp.dot(oa, wa_ref[...], preferred_element_type=F32)
    t2 = jnp.dot(ob, wb_ref[...], preferred_element_type=F32)
    ga = jnp.concatenate([ga0_ref[...], ga1_ref[...]], axis=-1).reshape(nb * r, 2 * GATE_BLK).astype(F32)
    gb = jnp.concatenate([gb0_ref[...], gb1_ref[...]], axis=-1).reshape(nb * r, 2 * GATE_BLK).astype(F32)
    mix = jax.nn.sigmoid(ga) * t1 + jax.nn.sigmoid(gb) * t2
    part = jnp.dot(mix.astype(BF16), wo_ref[...], preferred_element_type=F32).reshape(nb, r, d)

    @pl.when(j == 0)
    def _():
        x1_ref[...] = part

    @pl.when(j > 0)
    def _():
        x1_ref[...] += part

    @pl.when(j == pl.num_programs(1) - 1)
    def _():
        x1 = x_ref[...] + gt_ref[...] * x1_ref[...]
        x1_ref[...] = x1
        h2 = _rms_mod(x1, g2_ref[...], sc_ref[...], sh_ref[...])
        h2_ref[...] = h2.reshape(nb * r, d).astype(h2_ref.dtype)


def _outproj(oa, ob, z, x, gt, sc, sh, g2, wa, wb, wo, nb, r, h2_rows, h2_row0, h2_buf=None):
    b, s, d = x.shape
    nt = s // r
    tc = 2 * GATE_BLK
    ga0, gb0 = OFF_GA // GATE_BLK, (OFF_GA + d) // GATE_BLK
    h2_blk0 = h2_row0 // (nb * r)
    extra_in = [] if h2_buf is None else [h2_buf]
    extra_specs = [] if h2_buf is None else [pl.BlockSpec(memory_space=pl.ANY)]
    aliases = {} if h2_buf is None else {14: 1}

    def rows(i, j):
        return (i // nt, i % nt, 0)

    def per_stream(i, j):
        return (i // nt, 0, 0)

    def gate(blk0, k):
        return pl.BlockSpec((nb, r, GATE_BLK), lambda i, j: (i // nt, i % nt, blk0 + 2 * j + k))

    return pl.pallas_call(
        _outproj_kernel,
        grid=((b // nb) * nt, d // tc),
        in_specs=[pl.BlockSpec((nb, r, ATT_DIM), rows),
                  pl.BlockSpec((nb, r, GM_DIM), rows),
                  gate(ga0, 0), gate(ga0, 1), gate(gb0, 0), gate(gb0, 1),
                  pl.BlockSpec((nb, r, d), rows),
                  pl.BlockSpec((nb, 1, d), per_stream),
                  pl.BlockSpec((nb, 1, d), per_stream),
                  pl.BlockSpec((nb, 1, d), per_stream),
                  pl.BlockSpec((1, 1, d), lambda i, j: (0, 0, 0)),
                  pl.BlockSpec((ATT_DIM, tc), lambda i, j: (0, j)),
                  pl.BlockSpec((GM_DIM, tc), lambda i, j: (0, j)),
                  pl.BlockSpec((tc, d), lambda i, j: (j, 0))] + extra_specs,
        out_specs=[pl.BlockSpec((nb, r, d), rows),
                   pl.BlockSpec((nb * r, d), lambda i, j: (h2_blk0 + i, 0))],
        out_shape=[jax.ShapeDtypeStruct((b, s, d), F32), jax.ShapeDtypeStruct((h2_rows, d), BF16)],
        input_output_aliases=aliases,
        compiler_params=_params(("parallel", "arbitrary")),
        name="outproj",
    )(oa, ob, z, z, z, z, x, gt, sc, sh, g2.reshape(1, 1, d), wa, wb, wo, *extra_in)


def _peer_select_kernel(h_ref, wq_ref, keys_ref, c1_ref, f1_ref, r2_ref, f2_ref,
                        s_ref, work_ref, rank_ref, sv_ref, cand_ref, fid_ref, cv_ref, cnt_ref, z_ref):
    tm = h_ref.shape[0]
    neg_inf = jnp.float32(-jnp.inf)
    q = jnp.dot(h_ref[...], wq_ref[...], preferred_element_type=F32).astype(BF16)
    row = lax.broadcasted_iota(I32, (N_KEYS, tm), 0)

    def n_tokens_not_topk(selected):
        n_sel = jnp.sum(selected.astype(I32), axis=0, keepdims=True)
        return jnp.sum((n_sel != PK_TOPK).astype(I32))

    ties = []

    for p in range(2):
        s = lax.dot_general(keys_ref[0, p], q[:, p * PK_HALF:(p + 1) * PK_HALF], NT_DIMS,
                            preferred_element_type=F32)
        s_ref[p] = s

        def next_max(k, m_prev, p=p):
            w = s_ref[p]
            m = jnp.max(jnp.where(w < m_prev, w, neg_inf), axis=0, keepdims=True)
            sv_ref[p, pl.ds(k, 1), :] = m
            return m

        m_last = lax.fori_loop(0, PK_TOPK, next_max, jnp.full((1, tm), jnp.inf, F32))
        if p == 1:
            rank = jnp.zeros((N_KEYS, tm), I32)
            for k in range(PK_TOPK):
                rank = jnp.where(sv_ref[p, k:k + 1, :] > s, k + 1, rank)
            rank_ref[p] = rank
        ties.append(n_tokens_not_topk(s >= m_last) > 0)

        @pl.when(ties[p])
        def _(p=p):
            def extract_one(k, carry):
                w = work_ref[...]
                m = jnp.max(w, axis=0, keepdims=True)
                idx = jnp.min(jnp.where(w == m, row, N_KEYS), axis=0, keepdims=True)
                hit = row == idx
                work_ref[...] = jnp.where(hit, neg_inf, w)
                rank_ref[p] = jnp.where(hit, k, rank_ref[p])
                sv_ref[p, pl.ds(k, 1), :] = m
                return carry

            work_ref[...] = s_ref[p]
            rank_ref[p] = jnp.full((N_KEYS, tm), PK_TOPK, I32)
            lax.fori_loop(0, PK_TOPK, extract_one, 0)

    sv1 = sv_ref[0]
    sv2 = sv_ref[1]
    r8 = lax.broadcasted_iota(I32, (8, tm), 0)
    r16 = lax.broadcasted_iota(I32, (16, tm), 0)
    off = 0
    for k1, nrows, nused in _CAND_GROUPS:
        rr = r16 if nrows == 16 else r8
        val = sv1[k1:k1 + 1, :] + sv2[0:nrows, :]
        cand_ref[off:off + nrows, :] = jnp.where(rr < nused, val, neg_inf)
        off += nrows
    cand_ref[off:off + 8, :] = sv1[8:16, :] + sv2[0:1, :]
    m0 = sv1[0:1, :] + sv2[0:1, :]

    def next_max2(k, m_prev):
        w = cand_ref[...]
        m = jnp.max(jnp.where(w < m_prev, w, neg_inf), axis=0, keepdims=True)
        cv_ref[pl.ds(k, 1), :] = m
        return m

    tau = lax.fori_loop(0, PK_TOPK, next_max2, jnp.full((1, tm), jnp.inf, F32))
    picked = cand_ref[...] >= tau
    counts = []
    off = 0
    for k1, nrows, nused in _CAND_GROUPS:
        counts.append(jnp.sum(picked[off:off + nrows].astype(I32), axis=0, keepdims=True))
        off += nrows
    counts.append(picked[off:off + 8].astype(I32))
    cnt_ref[...] = jnp.concatenate(counts, axis=0)
    z_ref[...] = jnp.sum(jnp.exp(cv_ref[...] - m0), axis=0, keepdims=True)

    @pl.when(n_tokens_not_topk(picked) > 0)
    def _():
        off = 0
        for k1, nrows, nused in _CAND_GROUPS:
            rr = r16 if nrows == 16 else r8
            fid_ref[off:off + nrows, :] = k1 * PK_TOPK + rr
            off += nrows
        fid_ref[off:off + 8, :] = (r8 + 8) * PK_TOPK
        big = jnp.int32(PK_TOPK * PK_TOPK)

        def extract2(k, carry):
            cnt, zsum = carry
            w = cand_ref[...]
            fid = fid_ref[...]
            m = jnp.max(w, axis=0, keepdims=True)
            idx = jnp.min(jnp.where(w == m, fid, big), axis=0, keepdims=True)
            cand_ref[...] = jnp.where(fid == idx, neg_inf, w)
            cnt = cnt + (r16 == lax.shift_right_logical(idx, 4)).astype(I32)
            zsum = zsum + jnp.exp(m - m0)
            return cnt, zsum

        cnt, zsum = lax.fori_loop(0, PK_TOPK, extract2,
                                  (jnp.zeros((PK_TOPK, tm), I32), jnp.zeros((1, tm), F32)))
        cnt_ref[...] = cnt
        z_ref[...] = zsum

    cnt = cnt_ref[...].astype(F32)
    s1 = s_ref[0]
    c1 = jnp.zeros((N_KEYS, tm), F32)
    for k1 in range(PK_TOPK):
        c1 = jnp.where(s1 == sv_ref[0, k1:k1 + 1, :], cnt[k1:k1 + 1, :], c1)
    c1_ref[0] = c1

    @pl.when(ties[0])
    def _():
        rank1 = rank_ref[0]
        c1 = jnp.zeros((N_KEYS, tm), F32)
        for k1 in range(PK_TOPK):
            c1 = jnp.where(rank1 == k1, cnt[k1:k1 + 1, :], c1)
        c1_ref[0] = c1

    f1_ref[0] = jnp.exp(s_ref[0] - sv_ref[0, 0:1, :])
    r2_ref[0] = rank_ref[1].astype(r2_ref.dtype)
    f2_ref[0] = (jnp.exp(s_ref[1] - sv_ref[1, 0:1, :]) / z_ref[...]).astype(f2_ref.dtype)


def _peer_select(h2, wq, keys, tm):
    t, d = h2.shape
    hw = 2 * PK_HALF
    sel = jax.ShapeDtypeStruct((PK_HEADS, N_KEYS, t), F32)
    selb = jax.ShapeDtypeStruct((PK_HEADS, N_KEYS, t), BF16)
    ospec = pl.BlockSpec((1, N_KEYS, tm), lambda i, h: (h, 0, i))
    return pl.pallas_call(
        _peer_select_kernel,
        grid=(t // tm, PK_HEADS),
        in_specs=[pl.BlockSpec((tm, d), lambda i, h: (i, 0)),
                  pl.BlockSpec((d, hw), lambda i, h: (0, h)),
                  pl.BlockSpec((1, 2, N_KEYS, PK_HALF), lambda i, h: (h, 0, 0, 0))],
        out_specs=[ospec, ospec, ospec, ospec],
        out_shape=[sel, sel, selb, selb],
        scratch_shapes=[pltpu.VMEM((2, N_KEYS, tm), F32),
                        pltpu.VMEM((N_KEYS, tm), F32),
                        pltpu.VMEM((2, N_KEYS, tm), I32),
                        pltpu.VMEM((2, PK_TOPK, tm), F32),
                        pltpu.VMEM((_CAND_ROWS, tm), F32),
                        pltpu.VMEM((_CAND_ROWS, tm), I32),
                        pltpu.VMEM((PK_TOPK, tm), F32),
                        pltpu.VMEM((PK_TOPK, tm), I32),
                        pltpu.VMEM((1, tm), F32)],
        compiler_params=_params(("parallel", "arbitrary")),
        name="peer_select",
    )(h2, wq, keys)


def _peer_dense_kernel(h_ref, c1_ref, f1_ref, r2_ref, f2_ref, u_ref, v_ref, y_ref, g_ref):
    j = pl.program_id(1)
    te = u_ref.shape[0]
    tm = h_ref.shape[0]

    @pl.when(j == 0)
    def _():
        y_ref[...] = jnp.zeros(y_ref.shape, y_ref.dtype)

    act = lax.dot_general(u_ref[...], h_ref[...], NT_DIMS, preferred_element_type=F32)
    for a in range(te // N_KEYS):
        i1 = j * (te // N_KEYS) + a
        w = jnp.zeros((N_KEYS, tm), BF16)
        for h in range(PK_HEADS):
            c1 = c1_ref[h, pl.ds(i1, 1), :].astype(BF16)
            f1 = f1_ref[h, pl.ds(i1, 1), :].astype(BF16)
            w = w + jnp.where(r2_ref[h] < c1, f2_ref[h] * f1, jnp.zeros((), BF16))
        rows = slice(a * N_KEYS, (a + 1) * N_KEYS)
        g_ref[rows, :] = jax.nn.gelu(act[rows, :].astype(BF16)) * w

    y_ref[...] += lax.dot_general(g_ref[...], v_ref[...], TN_DIMS, preferred_element_type=F32)


def _peer_dense(h2, sel, pu, pv, tm, te):
    t, d = h2.shape
    sspec = pl.BlockSpec((PK_HEADS, N_KEYS, tm), lambda i, j: (0, 0, i))
    return pl.pallas_call(
        _peer_dense_kernel,
        grid=(t // tm, pu.shape[0] // te),
        in_specs=[pl.BlockSpec((tm, d), lambda i, j: (i, 0)),
                  sspec, sspec, sspec, sspec,
                  pl.BlockSpec((te, d), lambda i, j: (j, 0)),
                  pl.BlockSpec((te, d), lambda i, j: (j, 0))],
        out_specs=pl.BlockSpec((tm, d), lambda i, j: (i, 0)),
        out_shape=jax.ShapeDtypeStruct((t, d), F32),
        scratch_shapes=[pltpu.VMEM((te, tm), BF16)],
        compiler_params=_params(("parallel", "arbitrary")),
        name="peer_dense",
    )(h2, *sel, pu, pv)


def _residual_kernel(x_ref, y_ref, gt_ref, g_ref, o_ref, *, final):
    x2 = x_ref[...] + gt_ref[...] * y_ref[...].reshape(x_ref.shape)
    if final:
        x2 = x2 * lax.rsqrt(jnp.mean(x2 * x2, axis=-1, keepdims=True) + EPS) * g_ref[...]
    o_ref[...] = x2


def _residual(x1, y, y_row0, gt, g_final, nb, r, final):
    b, s, d = x1.shape
    nt = s // r
    y_blk0 = y_row0 // (nb * r)
    return pl.pallas_call(
        functools.partial(_residual_kernel, final=final),
        grid=((b // nb) * nt,),
        in_specs=[pl.BlockSpec((nb, r, d), lambda i: (i // nt, i % nt, 0)),
                  pl.BlockSpec((nb * r, d), lambda i: (y_blk0 + i, 0)),
                  pl.BlockSpec((nb, 1, d), lambda i: (i // nt, 0, 0)),
                  pl.BlockSpec((1, 1, d), lambda i: (0, 0, 0))],
        out_specs=pl.BlockSpec((nb, r, d), lambda i: (i // nt, i % nt, 0)),
        out_shape=jax.ShapeDtypeStruct((b, s, d), F32),
        compiler_params=_params(("parallel",)),
        name="residual",
    )(x1, y, gt, g_final.reshape(1, 1, d))


def _mixers(x, mods, cache, wts, nb, r, gm_rows, gm_tr, h2_rows, h2_row0, h2_buf=None):
    (g1, w_in, sinks, ln_g, ln_b, ws, gm_b, wa, wb, wo, g2) = wts
    sh1, sc1, gt1, sh2, sc2, _ = mods
    b, s, d = x.shape
    if cache is None:
        z, kv = _inproj(x, sc1, sh1, g1, w_in, nb, 2 * r, BF16)
        kv = kv[:, -WINDOW:]
    else:
        z, kv = _inproj(x, sc1, sh1, g1, w_in, nb, r, F32)
    k = kv[:, :, :KV_DIM].reshape(b, -1, N_KV, HEAD_DIM)
    v = kv[:, :, KV_DIM:].reshape(b, -1, N_KV, HEAD_DIM)
    if cache is None:
        o_a = _attn_prompt(z, sinks)
        o_b, _ = _gmlp(z, ln_g, ln_b, ws, gm_b, gm_rows, gm_tr, emit_vg=False)
        state = (k, v)
    else:
        ck, cv = cache
        o_a = _attn_sample(z, ck.reshape(b, ck.shape[1], KV_DIM), cv.reshape(b, cv.shape[1], KV_DIM), sinks)
        o_b, vg = _gmlp(z, ln_g, ln_b, ws[:, :gm_rows, :gm_rows], gm_b[:, :gm_rows], gm_rows, gm_tr,
                        emit_vg=True)
        state = (k, v, vg)
    x1, h2 = _outproj(o_a, o_b, z, x, gt1, sc2, sh2, g2, wa, wb, wo, nb, r, h2_rows, h2_row0, h2_buf)
    return x1, h2, state


def kernel(x_prompt, x_sample, cache_k, cache_v, c_prompt, c_sample, w_mod, b_mod, g_norm1, w_in, attn_sinks, gm_ln_g, gm_ln_b, gm_ws, gm_b, w_branch_a, w_branch_b, w_out, g_norm2, pk_wq, pk_keys, peer_u, peer_v, g_final):
    depth = w_mod.shape[0]
    bp, sp, d = x_prompt.shape
    bs, ts, _ = x_sample.shape
    assert bp == 1 and sp % 1024 == 0 and (bs * ts) % 512 == 0 and ts <= CHUNK and ts % 8 == 0
    xp, xs = x_prompt, x_sample
    nc = bp + bs
    npad = -(-nc // 8) * 8
    kp, vp, ksm, vsm, gsm = [], [], [], [], []
    for l in range(depth):
        c_all = jnp.pad(jnp.concatenate([c_prompt, c_sample], axis=0), ((0, npad - nc), (0, 0)))
        mod = _modulation(c_all, w_mod[l], b_mod[l])
        mods_p = tuple(m.reshape(bp, 1, d) for m in jnp.split(mod[:bp], 6, axis=-1))
        mods_s = tuple(m.reshape(bs, 1, d) for m in jnp.split(mod[bp:nc], 6, axis=-1))
        wts = (g_norm1[l], w_in[l].astype(BF16), attn_sinks[l], gm_ln_g[l], gm_ln_b[l], gm_ws[l], gm_b[l],
               w_branch_a[l].astype(BF16), w_branch_b[l].astype(BF16), w_out[l].astype(BF16), g_norm2[l])
        final = l == depth - 1
        nbs = 512 // ts
        n_tok = bp * sp + bs * ts
        xp1, h2, (k_p, v_p) = _mixers(xp, mods_p, None, wts, nb=1, r=512, gm_rows=GM_BLOCK, gm_tr=512,
                                      h2_rows=n_tok, h2_row0=0)
        xs1, h2, (k_s, v_s, g_s) = _mixers(xs, mods_s, (cache_k[l], cache_v[l]), wts, nb=nbs, r=ts,
                                           gm_rows=ts, gm_tr=ts, h2_rows=n_tok, h2_row0=bp * sp, h2_buf=h2)
        sel = _peer_select(h2, pk_wq[l].astype(BF16), pk_keys[l].astype(BF16), tm=512)
        y = _peer_dense(h2, sel, peer_u[l].astype(BF16), peer_v[l].astype(BF16), tm=512, te=1024)
        xp = _residual(xp1, y, 0, mods_p[5], g_final, 1, 512, final)
        xs = _residual(xs1, y, bp * sp, mods_s[5], g_final, nbs, ts, final)
        kp.append(k_p)
        vp.append(v_p)
        ksm.append(k_s)
        vsm.append(v_s)
        gsm.append(g_s)
    return (xp, xs, jnp.stack(kp), jnp.stack(vp), jnp.stack(ksm), jnp.stack(vsm), jnp.stack(gsm))
```
